```python
import math
import jax, jax.numpy as jnp
from jax import lax
import numpy as np

D_MODEL = 2048
BATCH = 4
SEQ = 2048
DEPTH = 4
DEC_BATCH = 8
DEC_SEQ = 1
PAST_LEN = 16384
PAGE_SIZE = 128

N_A_LAYERS = DEPTH // 2
N_B_LAYERS = DEPTH - N_A_LAYERS
N_DENSE = (DEPTH + 1) // 2
N_MOE = DEPTH // 2
GDN_HEADS = D_MODEL // 128
GDN_DK = 128
GDN_DV = 128
GDN_QK = GDN_HEADS * GDN_DK
GDN_V = GDN_HEADS * GDN_DV
CONV_W = 4
CONV_CH = 2 * GDN_QK + GDN_V
GDN_PROJ = 2 * GDN_QK + 2 * GDN_V + 2 * GDN_HEADS
CHUNK = 64
FOX_HEADS = D_MODEL // 128
FOX_DH = 128
FOX_W = FOX_HEADS * FOX_DH
QBLOCK = 128
D_FF = 256 * ((8 * D_MODEL // 3 + 255) // 256)
N_EXPERTS = 8
TOP_K = 2
D_FF_EXPERT = D_FF // 2
EPS = 1e-6

kernel_name = "yoco_gdn_fox_adaln_moe_step"


def rms_norm(x, gain):
    xf = x.astype(jnp.float32)
    y = xf * lax.rsqrt(jnp.mean(xf * xf, axis=-1, keepdims=True) + EPS)
    return (y * gain.astype(jnp.float32)).astype(x.dtype)


def l2_norm(x):
    xf = x.astype(jnp.float32)
    return (xf * lax.rsqrt(jnp.sum(xf * xf, axis=-1, keepdims=True) + EPS)).astype(x.dtype)


def modulate(x, gain, shift, scale):
    return rms_norm(x, gain) * (1 + scale[:, None, :]) + shift[:, None, :]


def causal_conv(buf, x, w):
    L = x.shape[1]
    xc = jnp.concatenate([buf.astype(x.dtype), x], axis=1)
    y = xc[:, 0:L] * w[0]
    for j in range(1, CONV_W):
        y = y + xc[:, j:j + L] * w[j]
    return jax.nn.silu(y), xc[:, L:]


def gated_delta_rule(q, k, v, g, beta, s0):
    out_dtype = v.dtype
    B, L, H, DK = k.shape
    DV = v.shape[-1]
    pad = (-L) % CHUNK
    n = (L + pad) // CHUNK

    def chunks(t):
        t = t.astype(jnp.float32)
        t = jnp.pad(t, [(0, 0), (0, pad)] + [(0, 0)] * (t.ndim - 2))
        t = t.reshape((B, n, CHUNK) + t.shape[2:])
        return jnp.moveaxis(jnp.moveaxis(t, 3, 2), 1, 0)

    q, k, v, g, beta = chunks(q), chunks(k), chunks(v), chunks(g), chunks(beta)
    q = q * DK ** -0.5
    gc = jnp.cumsum(g, axis=-1)
    idx = jnp.arange(CHUNK)
    incl = idx[:, None] >= idx[None, :]
    strict = idx[:, None] > idx[None, :]
    decay = jnp.exp(jnp.where(incl, gc[..., :, None] - gc[..., None, :], -jnp.inf))
    kb = k * beta[..., None]
    m = jnp.where(strict, jnp.einsum('nbhid,nbhjd->nbhij', kb, k) * decay, 0.0)
    eye = jnp.eye(CHUNK, dtype=jnp.float32)
    rhs = jnp.concatenate([v * beta[..., None], kb * jnp.exp(gc)[..., None]], axis=-1)
    sol = lax.linalg.triangular_solve(eye + m, rhs, left_side=True, lower=True, unit_diagonal=True)
    u, w = sol[..., :DV], sol[..., DV:]
    a_intra = jnp.where(incl, jnp.einsum('nbhid,nbhjd->nbhij', q, k) * decay, 0.0)

    def step(S, xs):
        q_i, k_i, u_i, w_i, gc_i, a_i = xs
        v_new = u_i - jnp.einsum('bhck,bhkv->bhcv', w_i, S)
        o = (jnp.einsum('bhck,bhkv->bhcv', q_i * jnp.exp(gc_i)[..., None], S)
             + jnp.einsum('bhij,bhjv->bhiv', a_i, v_new))
        g_last = gc_i[..., -1]
        S = (S * jnp.exp(g_last)[..., None, None]
             + jnp.einsum('bhck,bhcv->bhkv', k_i * jnp.exp(g_last[..., None] - gc_i)[..., None], v_new))
        return S, o

    S, o = lax.scan(step, s0.astype(jnp.float32), (q, k, u, w, gc, a_intra))
    o = jnp.moveaxis(jnp.moveaxis(o, 0, 1), 2, 3).reshape(B, n * CHUNK, H, DV)[:, :L]
    return o.astype(out_dtype), S.astype(s0.dtype)


def gdn_mixer(u, w_in, conv_w, a_log, dt_bias, norm_w, w_out, conv_buf, s0):
    B, L, _ = u.shape
    proj = u @ w_in
    qkv = proj[..., :CONV_CH]
    z = proj[..., CONV_CH:CONV_CH + GDN_V].reshape(B, L, GDN_HEADS, GDN_DV)
    b_logit = proj[..., CONV_CH + GDN_V:CONV_CH + GDN_V + GDN_HEADS].astype(jnp.float32)
    a_logit = proj[..., CONV_CH + GDN_V + GDN_HEADS:].astype(jnp.float32)
    qkv, new_buf = causal_conv(conv_buf, qkv, conv_w)
    q = l2_norm(qkv[..., :GDN_QK].reshape(B, L, GDN_HEADS, GDN_DK))
    k = l2_norm(qkv[..., GDN_QK:2 * GDN_QK].reshape(B, L, GDN_HEADS, GDN_DK))
    v = qkv[..., 2 * GDN_QK:].reshape(B, L, GDN_HEADS, GDN_DV)
    beta = jax.nn.sigmoid(b_logit)
    g = -jnp.exp(a_log.astype(jnp.float32)) * jax.nn.softplus(a_logit + dt_bias.astype(jnp.float32))
    o, s_new = gated_delta_rule(q, k, v, g, beta, s0)
    o = rms_norm(o, norm_w) * jax.nn.silu(z)
    return o.reshape(B, L, GDN_V) @ w_out, new_buf, s_new


def shared_kv(h, c_act, kv_norm, w_ada_kv, b_ada_kv, w_kvf, b_f, k_norm):
    B, L, _ = h.shape
    shift, scale = jnp.split(c_act @ w_ada_kv + b_ada_kv, 2, axis=-1)
    kvf = modulate(h, kv_norm, shift, scale) @ w_kvf
    k = rms_norm(kvf[..., :FOX_W].reshape(B, L, FOX_HEADS, FOX_DH), k_norm)
    v = kvf[..., FOX_W:2 * FOX_W].reshape(B, L, FOX_HEADS, FOX_DH)
    logf = jax.nn.log_sigmoid(kvf[..., 2 * FOX_W:].astype(jnp.float32) + b_f.astype(jnp.float32))
    return k, v, logf


def forgetting_attention(q, f_q, k_past, v_past, f_past, k_new, v_new):
    B, L, H, dh = q.shape
    P = k_past.shape[1]
    qb = min(QBLOCK, L)
    nb = -(-L // qb)
    pad = nb * qb - L
    q_blocks = jnp.moveaxis(jnp.pad(q, ((0, 0), (0, pad), (0, 0), (0, 0))).reshape(B, nb, qb, H, dh), 1, 0)
    f_blocks = jnp.moveaxis(jnp.pad(f_q, ((0, 0), (0, pad), (0, 0))).reshape(B, nb, qb, H), 1, 0)
    q_idx = jnp.arange(nb * qb).reshape(nb, qb)
    fk = jnp.concatenate([f_past, f_q], axis=1).transpose(0, 2, 1)
    new_idx = jnp.arange(L)
    scale = dh ** -0.5

    def block(args):
        qi, fi, ii = args
        s_p = jnp.einsum('bqhd,bkhd->bhqk', qi, k_past, preferred_element_type=jnp.float32)
        s_n = jnp.einsum('bqhd,bkhd->bhqk', qi, k_new, preferred_element_type=jnp.float32)
        s = (jnp.concatenate([s_p, s_n], axis=-1) * scale
             + fi.transpose(0, 2, 1)[..., None] - fk[:, :, None, :])
        visible = jnp.concatenate([jnp.ones((qb, P), dtype=bool), new_idx[None, :] <= ii[:, None]], axis=1)
        p = jax.nn.softmax(jnp.where(visible, s, -jnp.inf), axis=-1).astype(v_new.dtype)
        return (jnp.einsum('bhqk,bkhd->bqhd', p[..., :P], v_past)
                + jnp.einsum('bhqk,bkhd->bqhd', p[..., P:], v_new))

    o = lax.map(block, (q_blocks, f_blocks, q_idx))
    return jnp.moveaxis(o, 0, 1).reshape(B, nb * qb, H, dh)[:, :L]


def fox_mixer(u, w_q, q_norm, w_o, k_past, v_past, f_past, k_new, v_new, f_new):
    B, L, _ = u.shape
    q = rms_norm((u @ w_q).reshape(B, L, FOX_HEADS, FOX_DH), q_norm)
    o = forgetting_attention(q, f_new, k_past, v_past, f_past, k_new, v_new)
    return o.reshape(B, L, FOX_W) @ w_o


def swiglu(u, w_gu, w_down):
    gate, up = jnp.split(u @ w_gu, 2, axis=-1)
    return (jax.nn.silu(gate) * up) @ w_down


def moe_swiglu(u, w_router, b_router, w_gu, w_down):
    logits = jnp.einsum('bld,de->ble', u, w_router, preferred_element_type=jnp.float32) + b_router.astype(jnp.float32)
    top_val, top_idx = lax.top_k(logits, TOP_K)
    top_w = jax.nn.softmax(top_val, axis=-1)
    combine = jnp.einsum('blk,blke->ble', top_w, jax.nn.one_hot(top_idx, N_EXPERTS, dtype=jnp.float32)).astype(u.dtype)
    y = jnp.zeros_like(u)
    for e in range(N_EXPERTS):
        y = y + combine[..., e:e + 1] * swiglu(u, w_gu[e], w_down[e])
    return y


def gather_pages(pool, page_table):
    g = pool[page_table]
    return g.reshape((g.shape[0], g.shape[1] * g.shape[2]) + pool.shape[2:])


def trunk(x, c, conv_bufs, gdn_states, past_k, past_v, past_logf, p):
    c_act = jax.nn.silu(c)
    h = x
    new_conv, new_gdn = [], []
    k_new = v_new = logf_new = f_past = f_new = None
    for layer in range(DEPTH):
        mod = c_act @ p['w_ada'][layer] + p['b_ada'][layer]
        shift_m, scale_m, gate_m, shift_f, scale_f, gate_f = jnp.split(mod, 6, axis=-1)
        u = modulate(h, p['norm_mix'][layer], shift_m, scale_m)
        if layer < N_A_LAYERS:
            y, buf, s_new = gdn_mixer(u, p['gdn_w_in'][layer], p['gdn_conv_w'][layer], p['gdn_a_log'][layer],
                                      p['gdn_dt_bias'][layer], p['gdn_norm'][layer], p['gdn_w_out'][layer],
                                      conv_bufs[layer], gdn_states[layer])
            new_conv.append(buf)
            new_gdn.append(s_new)
        else:
            b = layer - N_A_LAYERS
            y = fox_mixer(u, p['fox_w_q'][b], p['fox_q_norm'][b], p['fox_w_o'][b],
                          past_k, past_v, f_past, k_new, v_new, f_new)
        h = h + gate_m[:, None, :] * y
        u = modulate(h, p['norm_ffn'][layer], shift_f, scale_f)
        j = layer // 2
        if layer % 2 == 0:
            y = swiglu(u, p['ffn_w_gu'][j], p['ffn_w_down'][j])
        else:
            y = moe_swiglu(u, p['moe_w_router'][j], p['moe_b_router'][j], p['moe_w_gu'][j], p['moe_w_down'][j])
        h = h + gate_f[:, None, :] * y
        if layer == N_A_LAYERS - 1:
            k_new, v_new, logf_new = shared_kv(h, c_act, p['kv_norm'], p['w_ada_kv'], p['b_ada_kv'],
                                               p['w_kvf'], p['b_f'], p['k_norm'])
            P = past_k.shape[1]
            f_cum = jnp.cumsum(jnp.concatenate([past_logf.astype(jnp.float32), logf_new], axis=1), axis=1)
            f_past, f_new = f_cum[:, :P], f_cum[:, P:]
    return h, k_new, v_new, logf_new, jnp.stack(new_gdn), jnp.stack(new_conv)


def setup_inputs(seed: int = 0) -> dict:
    key = jax.random.key(seed)
    ks = iter(jax.random.split(key, 64))

    def nrm(shape, scale=1.0):
        return jax.random.normal(next(ks), shape, jnp.float32) * scale

    def gain(shape):
        return 1.0 + nrm(shape, 0.02)

    d = D_MODEL
    n_pages = PAST_LEN // PAGE_SIZE
    n_phys = (DEC_BATCH * n_pages * 5) // 4
    dt = jnp.exp(jax.random.uniform(next(ks), (N_A_LAYERS, GDN_HEADS), jnp.float32, math.log(1e-3), math.log(1e-1)))
    dt_bias = dt + jnp.log(-jnp.expm1(-dt))
    a_log = jnp.log(jax.random.uniform(next(ks), (N_A_LAYERS, GDN_HEADS), jnp.float32, 1.0, 16.0))
    b_f = jax.random.uniform(next(ks), (FOX_HEADS,), jnp.float32, 1.0, 6.0)
    w_kvf = jnp.concatenate([nrm((d, FOX_W), d ** -0.5), nrm((d, FOX_W), d ** -0.5),
                             nrm((d, FOX_HEADS), 0.1 * d ** -0.5)], axis=1)
    page_table = jax.random.permutation(next(ks), n_phys)[:DEC_BATCH * n_pages].reshape(DEC_BATCH, n_pages).astype(jnp.int32)
    cache_logf = jax.nn.log_sigmoid(b_f[None, None, :] + nrm((n_phys, PAGE_SIZE, FOX_HEADS), 0.1))
    return {
        'x_prompt': nrm((BATCH, SEQ, d)),
        'x_sample': nrm((DEC_BATCH, DEC_SEQ, d)),
        'state_gdn': nrm((N_A_LAYERS, DEC_BATCH, GDN_HEADS, GDN_DK, GDN_DV), 0.1),
        'state_conv': nrm((N_A_LAYERS, DEC_BATCH, CONV_W - 1, CONV_CH)),
        'cache_k': nrm((n_phys, PAGE_SIZE, FOX_HEADS, FOX_DH)),
        'cache_v': nrm((n_phys, PAGE_SIZE, FOX_HEADS, FOX_DH)),
        'cache_logf': cache_logf,
        'page_table': page_table,
        'c_prompt': nrm((BATCH, d)),
        'c_sample': nrm((DEC_BATCH, d)),
        'w_ada': nrm((DEPTH, d, 6 * d), 0.5 * d ** -0.5),
        'b_ada': nrm((DEPTH, 6 * d), 0.02),
        'norm_mix': gain((DEPTH, d)),
        'norm_ffn': gain((DEPTH, d)),
        'gdn_w_in': nrm((N_A_LAYERS, d, GDN_PROJ), d ** -0.5),
        'gdn_conv_w': nrm((N_A_LAYERS, CONV_W, CONV_CH), CONV_W ** -0.5),
        'gdn_a_log': a_log,
        'gdn_dt_bias': dt_bias,
        'gdn_norm': gain((N_A_LAYERS, GDN_DV)),
        'gdn_w_out': nrm((N_A_LAYERS, GDN_V, d), GDN_V ** -0.5),
        'kv_norm': gain((d,)),
        'w_ada_kv': nrm((d, 2 * d), 0.5 * d ** -0.5),
        'b_ada_kv': nrm((2 * d,), 0.02),
        'w_kvf': w_kvf,
        'b_f': b_f,
        'k_norm': gain((FOX_DH,)),
        'fox_w_q': nrm((N_B_LAYERS, d, FOX_W), d ** -0.5),
        'fox_q_norm': gain((N_B_LAYERS, FOX_DH)),
        'fox_w_o': nrm((N_B_LAYERS, FOX_W, d), FOX_W ** -0.5),
        'ffn_w_gu': nrm((N_DENSE, d, 2 * D_FF), d ** -0.5),
        'ffn_w_down': nrm((N_DENSE, D_FF, d), D_FF ** -0.5),
        'moe_w_router': nrm((N_MOE, d, N_EXPERTS), d ** -0.5),
        'moe_b_router': nrm((N_MOE, N_EXPERTS), 0.01),
        'moe_w_gu': nrm((N_MOE, N_EXPERTS, d, 2 * D_FF_EXPERT), d ** -0.5),
        'moe_w_down': nrm((N_MOE, N_EXPERTS, D_FF_EXPERT, d), D_FF_EXPERT ** -0.5),
    }


def reference(x_prompt, x_sample, state_gdn, state_conv, cache_k, cache_v, cache_logf, page_table,
              c_prompt, c_sample, w_ada, b_ada, norm_mix, norm_ffn, gdn_w_in, gdn_conv_w, gdn_a_log,
              gdn_dt_bias, gdn_norm, gdn_w_out, kv_norm, w_ada_kv, b_ada_kv, w_kvf, b_f, k_norm,
              fox_w_q, fox_q_norm, fox_w_o, ffn_w_gu, ffn_w_down, moe_w_router, moe_b_router,
              moe_w_gu, moe_w_down):
    p = dict(w_ada=w_ada, b_ada=b_ada, norm_mix=norm_mix, norm_ffn=norm_ffn, gdn_w_in=gdn_w_in,
             gdn_conv_w=gdn_conv_w, gdn_a_log=gdn_a_log, gdn_dt_bias=gdn_dt_bias, gdn_norm=gdn_norm,
             gdn_w_out=gdn_w_out, kv_norm=kv_norm, w_ada_kv=w_ada_kv, b_ada_kv=b_ada_kv, w_kvf=w_kvf,
             b_f=b_f, k_norm=k_norm, fox_w_q=fox_w_q, fox_q_norm=fox_q_norm, fox_w_o=fox_w_o,
             ffn_w_gu=ffn_w_gu, ffn_w_down=ffn_w_down, moe_w_router=moe_w_router,
             moe_b_router=moe_b_router, moe_w_gu=moe_w_gu, moe_w_down=moe_w_down)
    bp = x_prompt.shape[0]
    dt = x_prompt.dtype
    empty_kv = jnp.zeros((bp, 0, FOX_HEADS, FOX_DH), dt)
    empty_f = jnp.zeros((bp, 0, FOX_HEADS), jnp.float32)
    y_prompt, k_p, v_p, f_p, s_p, cv_p = trunk(
        x_prompt, c_prompt,
        jnp.zeros((N_A_LAYERS, bp, CONV_W - 1, CONV_CH), state_conv.dtype),
        jnp.zeros((N_A_LAYERS, bp, GDN_HEADS, GDN_DK, GDN_DV), state_gdn.dtype),
        empty_kv, empty_kv, empty_f, p)
    past_k = gather_pages(cache_k, page_table)
    past_v = gather_pages(cache_v, page_table)
    past_f = gather_pages(cache_logf, page_table)
    y_sample, k_s, v_s, f_s, s_s, cv_s = trunk(x_sample, c_sample, state_conv, state_gdn,
                                              past_k, past_v, past_f, p)
    return (y_prompt, y_sample, k_p, v_p, f_p, s_p, cv_p, k_s, v_s, f_s, s_s, cv_s)
```

```python
import functools
import math

import jax
import jax.numpy as jnp
from jax import lax
from jax.experimental import pallas as pl
from jax.experimental.pallas import tpu as pltpu

F32 = jnp.float32
BF16 = jnp.bfloat16
EPS = 1e-6
HEAD = 128
CHUNK = 64
CONV_W = 4
N_EXPERTS = 8
HI = lax.Precision.HIGHEST
NEG_INF = float("-inf")


def _cparams(n_axes, vmem_mb):
    return pltpu.CompilerParams(dimension_semantics=("arbitrary",) * n_axes,
                                vmem_limit_bytes=vmem_mb << 20)


def _sigmoid(x):
    return 1.0 / (1.0 + jnp.exp(-x))


def _silu(x):
    return x * _sigmoid(x)


def _softplus(x):
    return jnp.maximum(x, 0.0) + jnp.log1p(jnp.exp(-jnp.abs(x)))


def _dot(a, b):
    return jnp.dot(a, b, preferred_element_type=F32)


def _dot_nt(a, b):
    return lax.dot_general(a, b, (((1,), (1,)), ((), ())), preferred_element_type=F32)


def _dot_tn(a, b, precision=None):
    return lax.dot_general(a, b, (((0,), (0,)), ((), ())), preferred_element_type=F32, precision=precision)


def _dot_hi(a, b):
    return jnp.dot(a, b, preferred_element_type=F32, precision=HI)


def _lane_pick(x, idx):
    lane = lax.broadcasted_iota(jnp.int32, x.shape, 1)
    return jnp.sum(jnp.where(lane == idx, x, 0.0), axis=1, keepdims=True)


def _row_pick(x, idx):
    sub = lax.broadcasted_iota(jnp.int32, x.shape, 0)
    return jnp.sum(jnp.where(sub == idx, x, 0.0), axis=0, keepdims=True)


def _matmul(x, ws, *, tm, tn, n_blocks, epilogue, outs, extras=(), silu_x=False, vmem_mb=48):
    M, K = x.shape
    nw, ne, no = len(ws), len(extras), len(outs)

    def kern(*refs):
        x_ref = refs[0]
        w_refs = refs[1:1 + nw]
        e_refs = refs[1 + nw:1 + nw + ne]
        o_refs = refs[1 + nw + ne:1 + nw + ne + no]
        wbf = refs[1 + nw + ne + no:]

        @pl.when(pl.program_id(1) == 0)
        def _():
            for wr, wb in zip(w_refs, wbf):
                wb[...] = wr[...].astype(BF16)

        xv = x_ref[...]
        if silu_x:
            xv = _silu(xv.astype(F32))
        xv = xv.astype(BF16)
        accs = [_dot(xv, wb[...]) for wb in wbf]
        res = epilogue(accs, [e[...] for e in e_refs])
        for o_ref, o in zip(o_refs, res):
            o_ref[...] = o.astype(o_ref.dtype)

    in_specs = [pl.BlockSpec((tm, K), lambda j, i: (i, 0))]
    args = [x]
    for arr, lead, off in ws:
        nl = len(lead)
        in_specs.append(pl.BlockSpec((None,) * nl + (K, tn),
                                     functools.partial(lambda j, i, lead, off: lead + (0, off + j), lead=tuple(lead), off=off)))
        args.append(arr)
    for arr, bs, im in extras:
        in_specs.append(pl.BlockSpec(bs, im))
        args.append(arr)
    res = pl.pallas_call(
        kern,
        grid=(n_blocks, M // tm),
        in_specs=in_specs,
        out_specs=[pl.BlockSpec(bs, im) for _, bs, im in outs],
        out_shape=[sd for sd, _, _ in outs],
        scratch_shapes=[pltpu.VMEM((K, tn), BF16) for _ in ws],
        compiler_params=_cparams(2, vmem_mb),
    )(*args)
    return res


def _pick_tm(M, K):
    if M <= 1024:
        return M
    return 1024 if K <= 2048 else 512


def _mm_plain(x, w, lead, *, n_cols, col0=0, tn, out_dtype=F32, bias=None, silu_x=False, mask_cols=None):
    M, K = x.shape
    tm = _pick_tm(M, K)
    nb = n_cols // tn
    extras = []
    if bias is not None:
        extras.append((bias, (1, tn), lambda j, i: (0, j)))

    def epi(accs, ev):
        y = accs[0]
        if bias is not None:
            y = y + ev[0]
        if mask_cols is not None:
            lane = lax.broadcasted_iota(jnp.int32, y.shape, 1)
            y = jnp.where(lane < mask_cols, y, 0.0)
        return (y,)

    (out,) = _matmul(x, [(w, lead, col0)], tm=tm, tn=tn, n_blocks=nb, epilogue=epi, silu_x=silu_x,
                     outs=[(jax.ShapeDtypeStruct((M, n_cols), out_dtype), (tm, tn), lambda j, i: (i, j))],
                     extras=extras)
    return out


def _head_rmsnorm(y, gain):
    parts = []
    for hh in range(y.shape[1] // HEAD):
        yh = y[:, hh * HEAD:(hh + 1) * HEAD]
        parts.append(yh * lax.rsqrt(jnp.mean(yh * yh, axis=-1, keepdims=True) + EPS) * gain)
    return jnp.concatenate(parts, axis=1) if len(parts) > 1 else parts[0]


def _mm_headnorm(x, w, lead, gain, *, n_cols, tn, out_dtypes):
    M, K = x.shape
    tm = _pick_tm(M, K)

    def epi(accs, ev):
        y = _head_rmsnorm(accs[0], ev[0])
        return tuple(y for _ in out_dtypes)

    return _matmul(x, [(w, lead, 0)], tm=tm, tn=tn, n_blocks=n_cols // tn, epilogue=epi,
                   outs=[(jax.ShapeDtypeStruct((M, n_cols), dt), (tm, tn), lambda j, i: (i, j)) for dt in out_dtypes],
                   extras=[(gain, (1, HEAD), lambda j, i: (0, 0))])


def _mm_swiglu(x, w, lead, *, d_ff, tn):
    M, K = x.shape
    tm = _pick_tm(M, K)
    nb = d_ff // tn

    def epi(accs, ev):
        return (_silu(accs[0]) * accs[1],)

    (out,) = _matmul(x, [(w, lead, 0), (w, lead, nb)], tm=tm, tn=tn, n_blocks=nb, epilogue=epi,
                     outs=[(jax.ShapeDtypeStruct((M, d_ff), BF16), (tm, tn), lambda j, i: (i, j))])
    return out


def _mm_moe_down(act, w, lead, comb, e, prev, *, tn):
    M, K = act.shape
    N = w.shape[-1]
    tm = _pick_tm(M, K)
    extras = [(comb, (tm, HEAD), lambda j, i: (i, 0))]
    if prev is not None:
        extras.append((prev, (tm, tn), lambda j, i: (i, j)))

    def epi(accs, ev):
        y = ev[0][:, e:e + 1] * accs[0]
        if prev is not None:
            y = ev[1] + y
        return (y,)

    (out,) = _matmul(act, [(w, lead, 0)], tm=tm, tn=tn, n_blocks=N // tn, epilogue=epi,
                     outs=[(jax.ShapeDtypeStruct((M, N), F32), (tm, tn), lambda j, i: (i, j))],
                     extras=extras)
    return out


def _resmod_kernel(*refs, has_y, has_mod):
    it = iter(refs)
    h_ref = next(it)
    if has_y:
        y_ref = next(it)
        gate_ref = next(it)
    if has_mod:
        gain_ref = next(it)
        shift_ref = next(it)
        scale_ref = next(it)
    if has_y:
        hn_ref = next(it)
    if has_mod:
        u_ref = next(it)
    h = h_ref[0]
    if has_y:
        h = h + gate_ref[0] * y_ref[0].astype(F32)
        hn_ref[0] = h
    if has_mod:
        ms = jnp.mean(h * h, axis=-1, keepdims=True)
        y = h * lax.rsqrt(ms + EPS) * gain_ref[...]
        u_ref[0] = (y * (1.0 + scale_ref[0]) + shift_ref[0]).astype(u_ref.dtype)


def _resmod(h, y=None, gate=None, gain=None, shift=None, scale=None):
    B, L, D = h.shape
    tl = min(L, 256)
    has_y, has_mod = y is not None, gain is not None

    def mod_spec(a):
        if a.shape[1] == 1:
            return pl.BlockSpec((1, 1, D), lambda b, t: (b, 0, 0))
        return pl.BlockSpec((1, tl, D), lambda b, t: (b, t, 0))

    row_spec = pl.BlockSpec((1, tl, D), lambda b, t: (b, t, 0))
    args, in_specs, out_shape, out_specs = [h], [row_spec], [], []
    if has_y:
        args += [y, gate]
        in_specs += [row_spec, mod_spec(gate)]
        out_shape.append(jax.ShapeDtypeStruct((B, L, D), F32))
        out_specs.append(row_spec)
    if has_mod:
        args += [gain, shift, scale]
        in_specs += [pl.BlockSpec((1, D), lambda b, t: (0, 0)), mod_spec(shift), mod_spec(scale)]
        out_shape.append(jax.ShapeDtypeStruct((B, L, D), BF16))
        out_specs.append(row_spec)
    res = pl.pallas_call(
        functools.partial(_resmod_kernel, has_y=has_y, has_mod=has_mod),
        grid=(B, L // tl), in_specs=in_specs, out_specs=out_specs, out_shape=out_shape,
        compiler_params=_cparams(2, 40),
    )(*args)
    return res if len(res) > 1 else res[0]


def _gdn_gates(gt, alog, dtb, nh):
    beta = _sigmoid(gt[:, 0:nh])
    g = -jnp.exp(alog) * _softplus(gt[:, nh:2 * nh] + dtb)
    return beta, g


def _gdn_prep_kernel(g_ref, alog_ref, dtb_ref, pcol_ref, grow_ref, *, nh):
    R = g_ref.shape[0]
    beta, g = _gdn_gates(g_ref[...], alog_ref[...], dtb_ref[...], nh)
    ii = lax.broadcasted_iota(jnp.int32, (R, R), 0)
    jj = lax.broadcasted_iota(jnp.int32, (R, R), 1)
    tri = jnp.where((jj <= ii) & ((ii // CHUNK) == (jj // CHUNK)), 1.0, 0.0).astype(F32)
    gc = _dot_hi(tri, g)
    pc = jnp.concatenate([beta, gc, jnp.zeros((R, HEAD - 2 * nh), F32)], axis=1)
    pcol_ref[...] = pc
    grow_ref[...] = pc.T[nh:2 * nh, :]


def _gdn_prep(gates, alog, dtb, nh):
    M = gates.shape[0]
    R = 256
    return pl.pallas_call(
        functools.partial(_gdn_prep_kernel, nh=nh),
        grid=(M // R,),
        in_specs=[pl.BlockSpec((R, HEAD), lambda r: (r, 0)),
                  pl.BlockSpec((1, nh), lambda r: (0, 0)),
                  pl.BlockSpec((1, nh), lambda r: (0, 0))],
        out_specs=[pl.BlockSpec((R, HEAD), lambda r: (r, 0)),
                   pl.BlockSpec((nh, R), lambda r: (0, r))],
        out_shape=[jax.ShapeDtypeStruct((M, HEAD), F32), jax.ShapeDtypeStruct((nh, M), F32)],
        compiler_params=_cparams(1, 32),
    )(gates, alog, dtb)


def _gdn_chunk_kernel(q_ref, k_ref, v_ref, z_ref, cwq_ref, cwk_ref, cwv_ref, pcol_ref, grow_ref, nw_ref,
                      o_ref, s_ref, xq, xk, xv, S, *, nh):
    R = q_ref.shape[0]
    h = pl.program_id(1)
    r = pl.program_id(2)
    PADR = 8

    @pl.when(r == 0)
    def _():
        zero = jnp.zeros((PADR, HEAD), F32)
        xq[0:PADR, :] = zero
        xk[0:PADR, :] = zero
        xv[0:PADR, :] = zero
        S[...] = jnp.zeros_like(S)

    @pl.when(r > 0)
    def _():
        xq[0:PADR, :] = xq[R:R + PADR, :]
        xk[0:PADR, :] = xk[R:R + PADR, :]
        xv[0:PADR, :] = xv[R:R + PADR, :]

    xq[PADR:PADR + R, :] = q_ref[...]
    xk[PADR:PADR + R, :] = k_ref[...]
    xv[PADR:PADR + R, :] = v_ref[...]

    def conv(xb, cw_ref):
        cw = cw_ref[...]
        y = xb[PADR - 3:PADR - 3 + R, :] * cw[0:1]
        for j in range(1, CONV_W):
            y = y + xb[PADR - 3 + j:PADR - 3 + j + R, :] * cw[j:j + 1]
        return _silu(y)

    def l2n(x):
        return x * lax.rsqrt(jnp.sum(x * x, axis=-1, keepdims=True) + EPS)

    q = l2n(conv(xq, cwq_ref)) * (HEAD ** -0.5)
    k = l2n(conv(xk, cwk_ref))
    v = conv(xv, cwv_ref)

    pc = pcol_ref[...]
    beta = _lane_pick(pc, h)
    gcc = _lane_pick(pc, nh + h)
    gcr = _row_pick(grow_ref[...], h)

    ii = lax.broadcasted_iota(jnp.int32, (CHUNK, CHUNK), 0)
    jj = lax.broadcasted_iota(jnp.int32, (CHUNK, CHUNK), 1)
    eye = jnp.where(ii == jj, 1.0, 0.0).astype(F32)
    nw = nw_ref[...]

    for c in range(R // CHUNK):
        sl = slice(c * CHUNK, (c + 1) * CHUNK)
        qc, kc, vc, bc, g_c = q[sl], k[sl], v[sl], beta[sl], gcc[sl]
        g_r = gcr[:, sl]
        decay = jnp.exp(jnp.where(ii >= jj, g_c - g_r, NEG_INF))
        kb = kc * bc
        kcb = kc.astype(BF16)
        m = jnp.where(ii > jj, _dot_nt(kb.astype(BF16), kcb) * decay, 0.0)
        t = eye - m
        xp = _dot_hi(m, m)
        for _ in range(4):
            t = t + _dot_hi(t, xp)
            xp = _dot_hi(xp, xp)
        t = t + _dot_hi(t, xp)
        rhs = jnp.concatenate([vc * bc, kb * jnp.exp(g_c)], axis=1)
        sol = _dot_hi(t, rhs)
        u, w = sol[:, :HEAD], sol[:, HEAD:]
        a = jnp.where(ii >= jj, _dot_nt(qc.astype(BF16), kcb) * decay, 0.0)
        Sb = S[...]
        Sbf = Sb.astype(BF16)
        v_new = u - _dot(w.astype(BF16), Sbf)
        vnb = v_new.astype(BF16)
        o = _dot((qc * jnp.exp(g_c)).astype(BF16), Sbf) + _dot(a.astype(BF16), vnb)
        g_last = g_c[CHUNK - 1:CHUNK, :]
        S[...] = Sb * jnp.exp(g_last) + _dot_tn((kc * jnp.exp(g_last - g_c)).astype(BF16), vnb)
        on = o * lax.rsqrt(jnp.mean(o * o, axis=-1, keepdims=True) + EPS) * nw
        o_ref[sl, :] = (on * _silu(z_ref[sl, :])).astype(o_ref.dtype)

    @pl.when(r == pl.num_programs(2) - 1)
    def _():
        s_ref[...] = S[...]


def _gdn_chunked(proj, conv_w, pcol, grow, norm_w, B, L, nh):
    M = B * L
    R = 256
    nr = L // R

    def col(off):
        return pl.BlockSpec((R, HEAD), lambda b, h, r: (b * nr + r, off * nh + h))

    def cw(off):
        return pl.BlockSpec((CONV_W, HEAD), lambda b, h, r: (0, off * nh + h))

    return pl.pallas_call(
        functools.partial(_gdn_chunk_kernel, nh=nh),
        grid=(B, nh, nr),
        in_specs=[col(0), col(1), col(2), col(3), cw(0), cw(1), cw(2),
                  pl.BlockSpec((R, HEAD), lambda b, h, r: (b * nr + r, 0)),
                  pl.BlockSpec((nh, R), lambda b, h, r: (0, b * nr + r)),
                  pl.BlockSpec((1, HEAD), lambda b, h, r: (0, 0))],
        out_specs=[pl.BlockSpec((R, HEAD), lambda b, h, r: (b * nr + r, h)),
                   pl.BlockSpec((None, None, HEAD, HEAD), lambda b, h, r: (b, h, 0, 0))],
        out_shape=[jax.ShapeDtypeStruct((M, nh * HEAD), BF16),
                   jax.ShapeDtypeStruct((B, nh, HEAD, HEAD), F32)],
        scratch_shapes=[pltpu.VMEM((R + 8, HEAD), F32)] * 3 + [pltpu.VMEM((HEAD, HEAD), F32)],
        compiler_params=_cparams(3, 32),
    )(proj, proj, proj, proj, conv_w, conv_w, conv_w, pcol, grow, norm_w)


def _gdn_step_kernel(q_ref, k_ref, v_ref, z_ref, cq_ref, ck_ref, cv_ref, cwq_ref, cwk_ref, cwv_ref,
                     g_ref, alog_ref, dtb_ref, nw_ref, s0_ref, o_ref, s_ref, *, nh):
    h = pl.program_id(1)

    def conv(c_ref, x_ref, cw_ref):
        cs, cw = c_ref[...], cw_ref[...]
        y = cs[0:1] * cw[0:1]
        for j in range(1, CONV_W - 1):
            y = y + cs[j:j + 1] * cw[j:j + 1]
        y = y + x_ref[...] * cw[CONV_W - 1:CONV_W]
        return _silu(y)

    def l2n(x):
        return x * lax.rsqrt(jnp.sum(x * x, axis=-1, keepdims=True) + EPS)

    q = l2n(conv(cq_ref, q_ref, cwq_ref)) * (HEAD ** -0.5)
    k = l2n(conv(ck_ref, k_ref, cwk_ref))
    v = conv(cv_ref, v_ref, cwv_ref)
    beta_all, g_all = _gdn_gates(g_ref[...], alog_ref[...], dtb_ref[...], nh)
    beta = _lane_pick(beta_all, h)
    g = _lane_pick(g_all, h)
    eg = jnp.exp(g)
    S0 = s0_ref[...]

    def rows8(x):
        return jnp.broadcast_to(x, (8, HEAD))

    kb = k * beta
    w = kb * eg
    v_new = v * beta - _dot_hi(rows8(w), S0)[0:1]
    a = jnp.sum(q * k, axis=-1, keepdims=True)
    o = _dot_hi(rows8(q * eg), S0)[0:1] + a * v_new
    s_ref[...] = S0 * eg + _dot_tn(rows8(k), rows8(v_new), precision=HI) * 0.125
    on = o * lax.rsqrt(jnp.mean(o * o, axis=-1, keepdims=True) + EPS) * nw_ref[...]
    o_ref[...] = on * _silu(z_ref[...])


def _gdn_step(proj, gates, conv_state, conv_w, alog, dtb, norm_w, s0, nh):
    B = proj.shape[0]
    proj3 = proj.reshape(B, 1, proj.shape[1])
    gates3 = gates.reshape(B, 1, HEAD)

    def col(off):
        return pl.BlockSpec((None, 1, HEAD), lambda b, h: (b, 0, off * nh + h))

    def cst(off):
        return pl.BlockSpec((None, CONV_W - 1, HEAD), lambda b, h: (b, 0, off * nh + h))

    def cw(off):
        return pl.BlockSpec((CONV_W, HEAD), lambda b, h: (0, off * nh + h))

    vec = pl.BlockSpec((1, nh), lambda b, h: (0, 0))
    o, s = pl.pallas_call(
        functools.partial(_gdn_step_kernel, nh=nh),
        grid=(B, nh),
        in_specs=[col(0), col(1), col(2), col(3), cst(0), cst(1), cst(2), cw(0), cw(1), cw(2),
                  pl.BlockSpec((None, 1, HEAD), lambda b, h: (b, 0, 0)), vec, vec,
                  pl.BlockSpec((1, HEAD), lambda b, h: (0, 0)),
                  pl.BlockSpec((None, None, HEAD, HEAD), lambda b, h: (b, h, 0, 0))],
        out_specs=[pl.BlockSpec((None, 1, HEAD), lambda b, h: (b, 0, h)),
                   pl.BlockSpec((None, None, HEAD, HEAD), lambda b, h: (b, h, 0, 0))],
        out_shape=[jax.ShapeDtypeStruct((B, 1, nh * HEAD), F32),
                   jax.ShapeDtypeStruct((B, nh, HEAD, HEAD), F32)],
        compiler_params=_cparams(2, 32),
    )(proj3, proj3, proj3, proj3, conv_state, conv_state, conv_state, conv_w, conv_w, conv_w,
      gates3, alog, dtb, norm_w, s0)
    return o.reshape(B, nh * HEAD), s


def _fcum_kernel(lf_ref, fcol_ref, frow_ref, carry, *, nh):
    R = lf_ref.shape[0]

    @pl.when(pl.program_id(1) == 0)
    def _():
        carry[...] = jnp.zeros_like(carry)

    ii = lax.broadcasted_iota(jnp.int32, (R, R), 0)
    jj = lax.broadcasted_iota(jnp.int32, (R, R), 1)
    tri = jnp.where(jj <= ii, 1.0, 0.0).astype(F32)
    f = _dot_hi(tri, lf_ref[...]) + carry[...]
    carry[...] = f[R - 1:R, :]
    fcol_ref[...] = f
    frow_ref[...] = f.T[0:nh, :]


def _fcum(logf, B, L, nh):
    R = 256
    nr = L // R
    return pl.pallas_call(
        functools.partial(_fcum_kernel, nh=nh),
        grid=(B, nr),
        in_specs=[pl.BlockSpec((R, HEAD), lambda b, r: (b * nr + r, 0))],
        out_specs=[pl.BlockSpec((R, HEAD), lambda b, r: (b * nr + r, 0)),
                   pl.BlockSpec((nh, R), lambda b, r: (0, b * nr + r))],
        out_shape=[jax.ShapeDtypeStruct((B * L, HEAD), F32), jax.ShapeDtypeStruct((nh, B * L), F32)],
        scratch_shapes=[pltpu.VMEM((1, HEAD), F32)],
        compiler_params=_cparams(2, 32),
    )(logf)


def _fox_prompt_kernel(q_ref, k_ref, v_ref, fcol_ref, frow_ref, o_ref, fr_s, m_s, l_s, acc_s):
    T = q_ref.shape[0]
    h = pl.program_id(1)
    qi = pl.program_id(2)
    scale = HEAD ** -0.5
    q = q_ref[...]
    fq = _lane_pick(fcol_ref[...], h)
    fr_s[...] = _row_pick(frow_ref[...], h)
    m_s[...] = jnp.full_like(m_s, NEG_INF)
    l_s[...] = jnp.zeros_like(l_s)
    acc_s[...] = jnp.zeros_like(acc_s)

    def block(ki, masked):
        off = pl.multiple_of(ki * T, T)
        kb = k_ref[pl.ds(off, T), :]
        vb = v_ref[pl.ds(off, T), :]
        s = _dot_nt(q, kb) * scale + fq - fr_s[:, pl.ds(off, T)]
        if masked:
            ii = lax.broadcasted_iota(jnp.int32, (T, T), 0)
            jj = lax.broadcasted_iota(jnp.int32, (T, T), 1)
            s = jnp.where(jj <= ii, s, NEG_INF)
        m_old = m_s[...]
        m_new = jnp.maximum(m_old, jnp.max(s, axis=-1, keepdims=True))
        alpha = jnp.exp(m_old - m_new)
        p = jnp.exp(s - m_new)
        l_s[...] = l_s[...] * alpha + jnp.sum(p, axis=-1, keepdims=True)
        acc_s[...] = acc_s[...] * alpha + _dot(p.astype(BF16), vb)
        m_s[...] = m_new

    def body(ki, c):
        block(ki, False)
        return c

    lax.fori_loop(0, qi, body, 0)
    block(qi, True)
    o_ref[...] = (acc_s[...] / l_s[...]).astype(o_ref.dtype)


def _fox_prompt(q, k, v, fcol, frow, B, L, nh):
    T = min(L, 512)
    nq = L // T
    return pl.pallas_call(
        _fox_prompt_kernel,
        grid=(B, nh, nq),
        in_specs=[pl.BlockSpec((T, HEAD), lambda b, h, i: (b * nq + i, h)),
                  pl.BlockSpec((L, HEAD), lambda b, h, i: (b, h)),
                  pl.BlockSpec((L, HEAD), lambda b, h, i: (b, h)),
                  pl.BlockSpec((T, HEAD), lambda b, h, i: (b * nq + i, 0)),
                  pl.BlockSpec((nh, L), lambda b, h, i: (0, b))],
        out_specs=pl.BlockSpec((T, HEAD), lambda b, h, i: (b * nq + i, h)),
        out_shape=jax.ShapeDtypeStruct((B * L, nh * HEAD), BF16),
        scratch_shapes=[pltpu.VMEM((1, L), F32), pltpu.VMEM((T, 1), F32), pltpu.VMEM((T, 1), F32),
                        pltpu.VMEM((T, HEAD), F32)],
        compiler_params=_cparams(3, 40),
    )(q, k, v, fcol, frow)


def _decode_bias_kernel(pt_ref, lf_ref, lfn_ref, b_ref, carry):
    P = lf_ref.shape[0]

    @pl.when(pl.program_id(1) == 0)
    def _():
        carry[...] = lfn_ref[...]

    lf = lf_ref[...]
    ii = lax.broadcasted_iota(jnp.int32, (P, P), 0)
    jj = lax.broadcasted_iota(jnp.int32, (P, P), 1)
    upper = jnp.where(jj > ii, 1.0, 0.0).astype(F32)
    b_ref[...] = _dot_hi(upper, lf) + carry[...]
    carry[...] = carry[...] + jnp.sum(lf, axis=0, keepdims=True)


def _decode_bias(cache_logf, page_table, logf_new):
    n_phys, P, nh = cache_logf.shape
    B, n_pages = page_table.shape
    lfn = logf_new.reshape(B, 1, nh)
    return pl.pallas_call(
        _decode_bias_kernel,
        grid_spec=pltpu.PrefetchScalarGridSpec(
            num_scalar_prefetch=1,
            grid=(B, n_pages),
            in_specs=[pl.BlockSpec((None, P, nh), lambda b, p, pt: (pt[b, n_pages - 1 - p], 0, 0)),
                      pl.BlockSpec((None, 1, nh), lambda b, p, pt: (b, 0, 0))],
            out_specs=pl.BlockSpec((None, None, P, nh), lambda b, p, pt: (b, n_pages - 1 - p, 0, 0)),
            scratch_shapes=[pltpu.VMEM((1, nh), F32)]),
        out_shape=jax.ShapeDtypeStruct((B, n_pages, P, nh), F32),
        compiler_params=_cparams(2, 32),
    )(page_table, cache_logf, lfn)


def _fox_decode_kernel(pt_ref, q_ref, k_ref, v_ref, b_ref, kn_ref, vn_ref, o_ref, m_s, l_s, acc_s, *, nh):
    p = pl.program_id(1)
    scale = HEAD ** -0.5

    @pl.when(p == 0)
    def _():
        m_s[...] = jnp.full_like(m_s, NEG_INF)
        l_s[...] = jnp.zeros_like(l_s)
        acc_s[...] = jnp.zeros_like(acc_s)

    q = q_ref[...]
    s = _dot_nt(q.astype(BF16), k_ref[...].astype(BF16)) * scale + b_ref[...]
    sub = lax.broadcasted_iota(jnp.int32, s.shape, 0)
    lane = lax.broadcasted_iota(jnp.int32, s.shape, 1)
    s = jnp.where((lane % nh) == sub, s, NEG_INF)
    m_old = m_s[...]
    m_new = jnp.maximum(m_old, jnp.max(s, axis=-1, keepdims=True))
    alpha = jnp.exp(m_old - m_new)
    pr = jnp.exp(s - m_new)
    l_s[...] = l_s[...] * alpha + jnp.sum(pr, axis=-1, keepdims=True)
    acc_s[...] = acc_s[...] * alpha + _dot(pr.astype(BF16), v_ref[...].astype(BF16))
    m_s[...] = m_new

    @pl.when(p == pl.num_programs(1) - 1)
    def _():
        qb = q.astype(BF16).astype(F32)
        kn = kn_ref[...].astype(BF16).astype(F32)
        s_n = jnp.sum(qb * kn, axis=-1, keepdims=True) * scale
        m_f = jnp.maximum(m_s[...], s_n)
        a = jnp.exp(m_s[...] - m_f)
        pn = jnp.exp(s_n - m_f)
        l_f = l_s[...] * a + pn
        pnb = pn.astype(BF16).astype(F32)
        vn = vn_ref[...].astype(BF16).astype(F32)
        o_ref[...] = (acc_s[...] * a + pnb * vn) / l_f


def _fox_decode(q, cache_k, cache_v, bias, k_new, v_new, page_table):
    n_phys, P, nh, dh = cache_k.shape
    B, n_pages = page_table.shape
    k2 = cache_k.reshape(n_phys, P * nh, dh)
    v2 = cache_v.reshape(n_phys, P * nh, dh)
    b4 = bias.reshape(B, n_pages, 1, P * nh)
    head_spec = pl.BlockSpec((None, nh, dh), lambda b, p, pt: (b, 0, 0))
    page_spec = pl.BlockSpec((None, P * nh, dh), lambda b, p, pt: (pt[b, p], 0, 0))
    return pl.pallas_call(
        functools.partial(_fox_decode_kernel, nh=nh),
        grid_spec=pltpu.PrefetchScalarGridSpec(
            num_scalar_prefetch=1,
            grid=(B, n_pages),
            in_specs=[head_spec, page_spec, page_spec,
                      pl.BlockSpec((None, None, 1, P * nh), lambda b, p, pt: (b, p, 0, 0)),
                      head_spec, head_spec],
            out_specs=head_spec,
            scratch_shapes=[pltpu.VMEM((nh, 1), F32), pltpu.VMEM((nh, 1), F32), pltpu.VMEM((nh, dh), F32)]),
        out_shape=jax.ShapeDtypeStruct((B, nh, dh), F32),
        compiler_params=_cparams(2, 32),
    )(page_table, q, k2, v2, b4, k_new, v_new)


def _router_kernel(u_ref, w_ref, b_ref, comb_ref):
    lg = _dot(u_ref[...].astype(BF16), w_ref[...].astype(BF16)) + b_ref[...]
    lane = lax.broadcasted_iota(jnp.int32, lg.shape, 1)
    lg = jnp.where(lane < N_EXPERTS, lg, NEG_INF)
    m1 = jnp.max(lg, axis=-1, keepdims=True)
    i1 = jnp.min(jnp.where(lg == m1, lane, HEAD), axis=-1, keepdims=True)
    lg2 = jnp.where(lane == i1, NEG_INF, lg)
    m2 = jnp.max(lg2, axis=-1, keepdims=True)
    i2 = jnp.min(jnp.where(lg2 == m2, lane, HEAD), axis=-1, keepdims=True)
    e = jnp.exp(m2 - m1)
    w1 = 1.0 / (1.0 + e)
    w2 = e / (1.0 + e)
    comb_ref[...] = jnp.where(lane == i1, w1, 0.0) + jnp.where(lane == i2, w2, 0.0)


def _router(u, w_router, b_router):
    M, D = u.shape
    tm = min(M, 512)
    wp = jnp.pad(w_router, ((0, 0), (0, HEAD - N_EXPERTS)))
    bp = jnp.pad(b_router, (0, HEAD - N_EXPERTS)).reshape(1, HEAD)
    return pl.pallas_call(
        _router_kernel,
        grid=(M // tm,),
        in_specs=[pl.BlockSpec((tm, D), lambda i: (i, 0)),
                  pl.BlockSpec((D, HEAD), lambda i: (0, 0)),
                  pl.BlockSpec((1, HEAD), lambda i: (0, 0))],
        out_specs=pl.BlockSpec((tm, HEAD), lambda i: (i, 0)),
        out_shape=jax.ShapeDtypeStruct((M, HEAD), F32),
        compiler_params=_cparams(1, 32),
    )(u, wp, bp)


def _ffn_tn(d_ff):
    for tn in (1024, 512, 256, 128):
        if d_ff % tn == 0:
            return tn
    raise ValueError(d_ff)


def _trunk(x, mods, mod_kv, p, *, prompt, conv_bufs=None, gdn_states=None,
           cache_k=None, cache_v=None, cache_logf=None, page_table=None):
    B, L, D = x.shape
    M = B * L
    depth = p['w_ada'].shape[0]
    n_a = p['gdn_w_in'].shape[0]
    nh = D // HEAD
    d_ff = p['ffn_w_down'].shape[1]
    d_ffe = p['moe_w_down'].shape[2]
    tn_d = 1024 if D % 1024 == 0 else D

    if prompt:
        def mod3(v):
            return v.reshape(B, 1, D)
        h = x
    else:
        def mod3(v):
            return v.reshape(1, B, D)
        h = x.reshape(1, B, D)

    def rows(a):
        return a.reshape(M, a.shape[-1])

    def as_h(a):
        return a.reshape(h.shape)

    def gain(v):
        return v.reshape(1, -1)

    new_conv, new_gdn = [], []
    k_new = v_new = logf_new = None
    y = gate = None
    for layer in range(depth):
        sh_m, sc_m, g_m, sh_f, sc_f, g_f = [mod3(mods[layer][:, i * D:(i + 1) * D]) for i in range(6)]
        if y is None:
            u = _resmod(h, gain=gain(p['norm_mix'][layer]), shift=sh_m, scale=sc_m)
        else:
            h, u = _resmod(h, y, gate, gain=gain(p['norm_mix'][layer]), shift=sh_m, scale=sc_m)
        u2 = rows(u)
        if layer < n_a:
            w_in = p['gdn_w_in']
            proj = _mm_plain(u2, w_in, (layer,), n_cols=4 * nh * HEAD, tn=tn_d)
            gates = _mm_plain(u2, w_in, (layer,), n_cols=HEAD, col0=4 * nh, tn=HEAD)
            alog = p['gdn_a_log'][layer].reshape(1, nh)
            dtb = p['gdn_dt_bias'][layer].reshape(1, nh)
            nw = p['gdn_norm'][layer].reshape(1, HEAD)
            cw = p['gdn_conv_w'][layer]
            n_conv = 3 * nh * HEAD
            if prompt:
                pcol, grow = _gdn_prep(gates, alog, dtb, nh)
                o, s_new = _gdn_chunked(proj, cw, pcol, grow, nw, B, L, nh)
                new_conv.append(proj.reshape(B, L, -1)[:, L - (CONV_W - 1):, :n_conv])
            else:
                o, s_new = _gdn_step(proj, gates, conv_bufs[layer], cw, alog, dtb, nw, gdn_states[layer], nh)
                new_conv.append(jnp.concatenate([conv_bufs[layer][:, 1:], proj[:, None, :n_conv]], axis=1))
            new_gdn.append(s_new)
            y = _mm_plain(o, p['gdn_w_out'], (layer,), n_cols=D, tn=tn_d)
        else:
            b = layer - n_a
            qn = gain(p['fox_q_norm'][b])
            if prompt:
                (q,) = _mm_headnorm(u2, p['fox_w_q'], (b,), qn, n_cols=D, tn=tn_d, out_dtypes=(BF16,))
                o = _fox_prompt(q, kbf, vbf, fcol, frow, B, L, nh)
            else:
                (q,) = _mm_headnorm(u2, p['fox_w_q'], (b,), qn, n_cols=D, tn=tn_d, out_dtypes=(F32,))
                o = _fox_decode(q.reshape(B, nh, HEAD), cache_k, cache_v, dec_bias,
                                k_new.reshape(B, nh, HEAD), v_new.reshape(B, nh, HEAD), page_table)
                o = o.reshape(B, nh * HEAD)
            y = _mm_plain(o, p['fox_w_o'], (b,), n_cols=D, tn=tn_d)
        h, u = _resmod(h, as_h(y), g_m, gain=gain(p['norm_ffn'][layer]), shift=sh_f, scale=sc_f)
        u2 = rows(u)
        j = layer // 2
        if layer % 2 == 0:
            act = _mm_swiglu(u2, p['ffn_w_gu'], (j,), d_ff=d_ff, tn=min(512, _ffn_tn(d_ff)))
            y = _mm_plain(act, p['ffn_w_down'], (j,), n_cols=D, tn=512 if D % 512 == 0 else D)
        else:
            comb = _router(u2, p['moe_w_router'][j], p['moe_b_router'][j])
            y = None
            for e in range(N_EXPERTS):
                act = _mm_swiglu(u2, p['moe_w_gu'], (j, e), d_ff=d_ffe, tn=min(512, _ffn_tn(d_ffe)))
                y = _mm_moe_down(act, p['moe_w_down'], (j, e), comb, e, y, tn=512 if D % 512 == 0 else D)
        y = as_h(y)
        gate = g_f
        if layer == n_a - 1:
            h = _resmod(h, y, gate)
            y = None
            shift, scale = mod3(mod_kv[:, :D]), mod3(mod_kv[:, D:])
            ukv = rows(_resmod(h, gain=gain(p['kv_norm']), shift=shift, scale=scale))
            w_kvf = p['w_kvf']
            k_new, kbf = _mm_headnorm(ukv, w_kvf, (), gain(p['k_norm']), n_cols=D, tn=tn_d, out_dtypes=(F32, BF16))
            v_new, vbf = _matmul(ukv, [(w_kvf, (), D // tn_d)], tm=_pick_tm(M, D), tn=tn_d, n_blocks=D // tn_d,
                                 epilogue=lambda accs, ev: (accs[0], accs[0]),
                                 outs=[(jax.ShapeDtypeStruct((M, D), dt), (_pick_tm(M, D), tn_d), lambda j, i: (i, j))
                                       for dt in (F32, BF16)])
            b_f = jnp.pad(p['b_f'], (0, HEAD - nh)).reshape(1, HEAD)
            logf_pad = _log_forget(ukv, w_kvf, b_f, col0=2 * D // HEAD, nh=nh)
            logf_new = logf_pad[:, :nh]
            if prompt:
                fcol, frow = _fcum(logf_pad, B, L, nh)
            else:
                dec_bias = _decode_bias(cache_logf, page_table, logf_new)
    h = _resmod(h, y, gate)
    return (h.reshape(B, L, D), k_new.reshape(B, L, nh, HEAD), v_new.reshape(B, L, nh, HEAD),
            logf_new.reshape(B, L, nh), jnp.stack(new_gdn), jnp.stack(new_conv))


def _log_forget(ukv, w_kvf, b_f, *, col0, nh):
    M, K = ukv.shape
    tm = _pick_tm(M, K)

    def epi(accs, ev):
        lane = lax.broadcasted_iota(jnp.int32, accs[0].shape, 1)
        return (jnp.where(lane < nh, -_softplus(-(accs[0] + ev[0])), 0.0),)

    (out,) = _matmul(ukv, [(w_kvf, (), col0)], tm=tm, tn=HEAD, n_blocks=1, epilogue=epi,
                     outs=[(jax.ShapeDtypeStruct((M, HEAD), F32), (tm, HEAD), lambda j, i: (i, 0))],
                     extras=[(b_f, (1, HEAD), lambda j, i: (0, 0))])
    return out


def kernel(x_prompt, x_sample, state_gdn, state_conv, cache_k, cache_v, cache_logf, page_table, c_prompt, c_sample, w_ada, b_ada, norm_mix, norm_ffn, gdn_w_in, gdn_conv_w, gdn_a_log, gdn_dt_bias, gdn_norm, gdn_w_out, kv_norm, w_ada_kv, b_ada_kv, w_kvf, b_f, k_norm, fox_w_q, fox_q_norm, fox_w_o, ffn_w_gu, ffn_w_down, moe_w_router, moe_b_router, moe_w_gu, moe_w_down):
    p = dict(w_ada=w_ada, b_ada=b_ada, norm_mix=norm_mix, norm_ffn=norm_ffn, gdn_w_in=gdn_w_in,
             gdn_conv_w=gdn_conv_w, gdn_a_log=gdn_a_log, gdn_dt_bias=gdn_dt_bias, gdn_norm=gdn_norm,
             gdn_w_out=gdn_w_out, kv_norm=kv_norm, w_ada_kv=w_ada_kv, b_ada_kv=b_ada_kv, w_kvf=w_kvf,
             b_f=b_f, k_norm=k_norm, fox_w_q=fox_w_q, fox_q_norm=fox_q_norm, fox_w_o=fox_w_o,
             ffn_w_gu=ffn_w_gu, ffn_w_down=ffn_w_down, moe_w_router=moe_w_router,
             moe_b_router=moe_b_router, moe_w_gu=moe_w_gu, moe_w_down=moe_w_down)
    bp, bs = c_prompt.shape[0], c_sample.shape[0]
    D = c_prompt.shape[1]
    depth = w_ada.shape[0]
    n_rows = -(-(bp + bs) // 8) * 8
    c_all = jnp.pad(jnp.concatenate([c_prompt, c_sample], axis=0), ((0, n_rows - bp - bs), (0, 0)))
    tn_a = 1024 if D % 1024 == 0 else D
    mods = [_mm_plain(c_all, w_ada, (l,), n_cols=6 * D, tn=tn_a, bias=b_ada[l].reshape(1, -1), silu_x=True)
            for l in range(depth)]
    mods_p = [m[:bp] for m in mods]
    mods_s = [m[bp:bp + bs] for m in mods]
    mod_kv = _mm_plain(c_all, w_ada_kv, (), n_cols=2 * D, tn=tn_a, bias=b_ada_kv.reshape(1, -1), silu_x=True)

    y_p, k_p, v_p, f_p, s_p, cv_p = _trunk(x_prompt, mods_p, mod_kv[:bp], p, prompt=True)
    y_s, k_s, v_s, f_s, s_s, cv_s = _trunk(x_sample, mods_s, mod_kv[bp:bp + bs], p, prompt=False,
                                           conv_bufs=state_conv, gdn_states=state_gdn, cache_k=cache_k,
                                           cache_v=cache_v, cache_logf=cache_logf, page_table=page_table)
    return (y_p, y_s, k_p, v_p, f_p, s_p, cv_p, k_s, v_s, f_s, s_s, cv_s)
```

```python
import functools
import math

import jax
import jax.numpy as jnp
from jax import lax
from jax.experimental import pallas as pl
from jax.experimental.pallas import tpu as pltpu

F32 = jnp.float32
BF16 = jnp.bfloat16
EPS = 1e-6
HEAD = 128
CHUNK = 64
CONV_W = 4
N_EXPERTS = 8
HI = lax.Precision.HIGHEST
NEG_INF = float("-inf")


def _cparams(n_axes, vmem_mb):
    return pltpu.CompilerParams(dimension_semantics=("arbitrary",) * n_axes,
                                vmem_limit_bytes=vmem_mb << 20)


def _sigmoid(x):
    return 1.0 / (1.0 + jnp.exp(-x))


def _silu(x):
    return x * _sigmoid(x)


def _softplus(x):
    return jnp.maximum(x, 0.0) + jnp.log1p(jnp.exp(-jnp.abs(x)))


def _dot(a, b):
    return jnp.dot(a, b, preferred_element_type=F32)


def _dot_nt(a, b):
    return lax.dot_general(a, b, (((1,), (1,)), ((), ())), preferred_element_type=F32)


def _dot_tn(a, b, precision=None):
    return lax.dot_general(a, b, (((0,), (0,)), ((), ())), preferred_element_type=F32, precision=precision)


def _dot_hi(a, b):
    return jnp.dot(a, b, preferred_element_type=F32, precision=HI)


def _lane_pick(x, idx):
    lane = lax.broadcasted_iota(jnp.int32, x.shape, 1)
    return jnp.sum(jnp.where(lane == idx, x, 0.0), axis=1, keepdims=True)


def _row_pick(x, idx):
    sub = lax.broadcasted_iota(jnp.int32, x.shape, 0)
    return jnp.sum(jnp.where(sub == idx, x, 0.0), axis=0, keepdims=True)


def _matmul(x, ws, *, tm, tn, n_blocks, epilogue, outs, extras=(), silu_x=False, vmem_mb=48):
    M, K = x.shape
    nw, ne, no = len(ws), len(extras), len(outs)

    def kern(*refs):
        x_ref = refs[0]
        w_refs = refs[1:1 + nw]
        e_refs = refs[1 + nw:1 + nw + ne]
        o_refs = refs[1 + nw + ne:1 + nw + ne + no]
        wbf = refs[1 + nw + ne + no:]

        @pl.when(pl.program_id(1) == 0)
        def _():
            for wr, wb in zip(w_refs, wbf):
                wb[...] = wr[...].astype(BF16)

        xv = x_ref[...]
        if silu_x:
            xv = _silu(xv.astype(F32))
        xv = xv.astype(BF16)
        accs = [_dot(xv, wb[...]) for wb in wbf]
        res = epilogue(accs, [e[...] for e in e_refs])
        for o_ref, o in zip(o_refs, res):
            o_ref[...] = o.astype(o_ref.dtype)

    in_specs = [pl.BlockSpec((tm, K), lambda j, i: (i, 0))]
    args = [x]
    for arr, lead, off in ws:
        nl = len(lead)
        in_specs.append(pl.BlockSpec((None,) * nl + (K, tn),
                                     functools.partial(lambda j, i, lead, off: lead + (0, off + j), lead=tuple(lead), off=off)))
        args.append(arr)
    for arr, bs, im in extras:
        in_specs.append(pl.BlockSpec(bs, im))
        args.append(arr)
    res = pl.pallas_call(
        kern,
        grid=(n_blocks, M // tm),
        in_specs=in_specs,
        out_specs=[pl.BlockSpec(bs, im) for _, bs, im in outs],
        out_shape=[sd for sd, _, _ in outs],
        scratch_shapes=[pltpu.VMEM((K, tn), BF16) for _ in ws],
        compiler_params=_cparams(2, vmem_mb),
    )(*args)
    return res


def _pick_tm(M, K):
    if M <= 1024:
        return M
    return 1024 if K <= 2048 else 512


def _mm_plain(x, w, lead, *, n_cols, col0=0, tn, out_dtype=F32, bias=None, silu_x=False, mask_cols=None):
    M, K = x.shape
    tm = _pick_tm(M, K)
    nb = n_cols // tn
    extras = []
    if bias is not None:
        extras.append((bias, (1, tn), lambda j, i: (0, j)))

    def epi(accs, ev):
        y = accs[0]
        if bias is not None:
            y = y + ev[0]
        if mask_cols is not None:
            lane = lax.broadcasted_iota(jnp.int32, y.shape, 1)
            y = jnp.where(lane < mask_cols, y, 0.0)
        return (y,)

    (out,) = _matmul(x, [(w, lead, col0)], tm=tm, tn=tn, n_blocks=nb, epilogue=epi, silu_x=silu_x,
                     outs=[(jax.ShapeDtypeStruct((M, n_cols), out_dtype), (tm, tn), lambda j, i: (i, j))],
                     extras=extras)
    return out


def _head_rmsnorm(y, gain):
    parts = []
    for hh in range(y.shape[1] // HEAD):
        yh = y[:, hh * HEAD:(hh + 1) * HEAD]
        parts.append(yh * lax.rsqrt(jnp.mean(yh * yh, axis=-1, keepdims=True) + EPS) * gain)
    return jnp.concatenate(parts, axis=1) if len(parts) > 1 else parts[0]


def _mm_headnorm(x, w, lead, gain, *, n_cols, tn, out_dtypes):
    M, K = x.shape
    tm = _pick_tm(M, K)

    def epi(accs, ev):
        y = _head_rmsnorm(accs[0], ev[0])
        return tuple(y for _ in out_dtypes)

    return _matmul(x, [(w, lead, 0)], tm=tm, tn=tn, n_blocks=n_cols // tn, epilogue=epi,
                   outs=[(jax.ShapeDtypeStruct((M, n_cols), dt), (tm, tn), lambda j, i: (i, j)) for dt in out_dtypes],
                   extras=[(gain, (1, HEAD), lambda j, i: (0, 0))])


def _mm_swiglu(x, w, lead, *, d_ff, tn):
    M, K = x.shape
    tm = _pick_tm(M, K)
    nb = d_ff // tn

    def epi(accs, ev):
        return (_silu(accs[0]) * accs[1],)

    (out,) = _matmul(x, [(w, lead, 0), (w, lead, nb)], tm=tm, tn=tn, n_blocks=nb, epilogue=epi,
                     outs=[(jax.ShapeDtypeStruct((M, d_ff), BF16), (tm, tn), lambda j, i: (i, j))])
    return out


def _mm_moe_down(act, w, lead, comb, e, prev, *, tn):
    M, K = act.shape
    N = w.shape[-1]
    tm = _pick_tm(M, K)
    extras = [(comb, (tm, HEAD), lambda j, i: (i, 0))]
    if prev is not None:
        extras.append((prev, (tm, tn), lambda j, i: (i, j)))

    def epi(accs, ev):
        y = ev[0][:, e:e + 1] * accs[0]
        if prev is not None:
            y = ev[1] + y
        return (y,)

    (out,) = _matmul(act, [(w, lead, 0)], tm=tm, tn=tn, n_blocks=N // tn, epilogue=epi,
                     outs=[(jax.ShapeDtypeStruct((M, N), F32), (tm, tn), lambda j, i: (i, j))],
                     extras=extras)
    return out


def _resmod_kernel(*refs, has_y, has_mod):
    it = iter(refs)
    h_ref = next(it)
    if has_y:
        y_ref = next(it)
        gate_ref = next(it)
    if has_mod:
        gain_ref = next(it)
        shift_ref = next(it)
        scale_ref = next(it)
    if has_y:
        hn_ref = next(it)
    if has_mod:
        u_ref = next(it)
    h = h_ref[0]
    if has_y:
        h = h + gate_ref[0] * y_ref[0].astype(F32)
        hn_ref[0] = h
    if has_mod:
        ms = jnp.mean(h * h, axis=-1, keepdims=True)
        y = h * lax.rsqrt(ms + EPS) * gain_ref[...]
        u_ref[0] = (y * (1.0 + scale_ref[0]) + shift_ref[0]).astype(u_ref.dtype)


def _resmod(h, y=None, gate=None, gain=None, shift=None, scale=None, u_dtype=None):
    B, L, D = h.shape
    tl = min(L, 256)
    has_y, has_mod = y is not None, gain is not None

    def mod_spec(a):
        if a.shape[1] == 1:
            return pl.BlockSpec((1, 1, D), lambda b, t: (b, 0, 0))
        return pl.BlockSpec((1, tl, D), lambda b, t: (b, t, 0))

    row_spec = pl.BlockSpec((1, tl, D), lambda b, t: (b, t, 0))
    args, in_specs, out_shape, out_specs = [h], [row_spec], [], []
    if has_y:
        args += [y, gate]
        in_specs += [row_spec, mod_spec(gate)]
        out_shape.append(jax.ShapeDtypeStruct((B, L, D), F32))
        out_specs.append(row_spec)
    if has_mod:
        args += [gain, shift, scale]
        in_specs += [pl.BlockSpec((1, D), lambda b, t: (0, 0)), mod_spec(shift), mod_spec(scale)]
        out_shape.append(jax.ShapeDtypeStruct((B, L, D), u_dtype or BF16))
        out_specs.append(row_spec)
    res = pl.pallas_call(
        functools.partial(_resmod_kernel, has_y=has_y, has_mod=has_mod),
        grid=(B, L // tl), in_specs=in_specs, out_specs=out_specs, out_shape=out_shape,
        compiler_params=_cparams(2, 40),
        name="resmod",
    )(*args)
    return res if len(res) > 1 else res[0]


def _gdn_gates(gt, alog, dtb, nh):
    beta = _sigmoid(gt[:, 0:nh])
    g = -jnp.exp(alog) * _softplus(gt[:, nh:2 * nh] + dtb)
    return beta, g


def _gdn_prep_kernel(g_ref, alog_ref, dtb_ref, pcol_ref, grow_ref, *, nh):
    R = g_ref.shape[0]
    beta, g = _gdn_gates(g_ref[...], alog_ref[...], dtb_ref[...], nh)
    ii = lax.broadcasted_iota(jnp.int32, (R, R), 0)
    jj = lax.broadcasted_iota(jnp.int32, (R, R), 1)
    tri = jnp.where((jj <= ii) & ((ii // CHUNK) == (jj // CHUNK)), 1.0, 0.0).astype(F32)
    gc = _dot_hi(tri, g)
    pc = jnp.concatenate([beta, gc, jnp.zeros((R, HEAD - 2 * nh), F32)], axis=1)
    pcol_ref[...] = pc
    pct = pc.T
    for hh in range(nh):
        for c in range(R // CHUNK):
            grow_ref[hh, c] = pct[nh + hh:nh + hh + 1, c * CHUNK:(c + 1) * CHUNK]


def _gdn_prep(gates, alog, dtb, nh):
    M = gates.shape[0]
    R = 256
    return pl.pallas_call(
        functools.partial(_gdn_prep_kernel, nh=nh),
        grid=(M // R,),
        in_specs=[pl.BlockSpec((R, HEAD), lambda r: (r, 0)),
                  pl.BlockSpec((1, nh), lambda r: (0, 0)),
                  pl.BlockSpec((1, nh), lambda r: (0, 0))],
        out_specs=[pl.BlockSpec((R, HEAD), lambda r: (r, 0)),
                   pl.BlockSpec((nh, R // CHUNK, 1, CHUNK), lambda r: (0, r, 0, 0))],
        out_shape=[jax.ShapeDtypeStruct((M, HEAD), F32), jax.ShapeDtypeStruct((nh, M // CHUNK, 1, CHUNK), F32)],
        compiler_params=_cparams(1, 32),
        name="gdn_prep",
    )(gates, alog, dtb)


def _bdot(a, b):
    return jnp.einsum('cij,cjk->cik', a.astype(BF16), b.astype(BF16), preferred_element_type=F32)


def _bdot_nt(a, b):
    return jnp.einsum('cik,cjk->cij', a.astype(BF16), b.astype(BF16), preferred_element_type=F32)


def _split(a):
    hi = a.astype(BF16)
    return hi, (a - hi.astype(F32)).astype(BF16)


def _bdot3(a, b):
    (ah, al), (bh, bl) = _split(a), _split(b)

    def mm(x, y):
        return jnp.einsum('cij,cjk->cik', x, y, preferred_element_type=F32)

    return mm(ah, bh) + (mm(ah, bl) + mm(al, bh))


def _unit_lower_inverse(m, ii, jj):
    bi, bj = ii // 16, jj // 16
    md = jnp.where(bi == bj, m, 0.0)
    t = jnp.where(ii == jj, 1.0, 0.0) - md
    xp = _bdot3(md, md)
    for _ in range(2):
        t = t + _bdot3(t, xp)
        xp = _bdot3(xp, xp)
    t = t + _bdot3(t, xp)
    off16 = jnp.where((ii // 32 == jj // 32) & (bi == bj + 1), m, 0.0)
    t = t - _bdot3(t, _bdot3(off16, t))
    off32 = jnp.where((ii >= 32) & (jj < 32), m, 0.0)
    return t - _bdot3(t, _bdot3(off32, t))


def _gdn_chunk_kernel(q_ref, k_ref, v_ref, z_ref, cwq_ref, cwk_ref, cwv_ref, pcol_ref, grow_ref, nw_ref,
                      o_ref, s_ref, xq, xk, xv, S, u_s, w_s, qg_s, kd_s, a_s, o_s, *, nh, hb):
    R = q_ref.shape[0]
    nc = R // CHUNK
    nb = hb * nc
    hg = pl.program_id(1)
    r = pl.program_id(2)
    PADR = 8

    @pl.when(r == 0)
    def _():
        zero = jnp.zeros((PADR, hb * HEAD), F32)
        xq[0:PADR, :] = zero
        xk[0:PADR, :] = zero
        xv[0:PADR, :] = zero
        S[...] = jnp.zeros_like(S)

    @pl.when(r > 0)
    def _():
        xq[0:PADR, :] = xq[R:R + PADR, :]
        xk[0:PADR, :] = xk[R:R + PADR, :]
        xv[0:PADR, :] = xv[R:R + PADR, :]

    xq[PADR:PADR + R, :] = q_ref[...]
    xk[PADR:PADR + R, :] = k_ref[...]
    xv[PADR:PADR + R, :] = v_ref[...]

    def conv(xb, cw_ref):
        cw = cw_ref[...]
        y = xb[PADR - 3:PADR - 3 + R, :] * cw[0:1]
        for j in range(1, CONV_W):
            y = y + xb[PADR - 3 + j:PADR - 3 + j + R, :] * cw[j:j + 1]
        return _silu(y)

    def chunks(x):
        return jnp.concatenate([x[:, i * HEAD:(i + 1) * HEAD].reshape(nc, CHUNK, HEAD) for i in range(hb)], axis=0)

    def l2n(x):
        return x * lax.rsqrt(jnp.sum(x * x, axis=-1, keepdims=True) + EPS)

    q3 = l2n(chunks(conv(xq, cwq_ref))) * (HEAD ** -0.5)
    k3 = l2n(chunks(conv(xk, cwk_ref)))
    v3 = chunks(conv(xv, cwv_ref))
    pc = pcol_ref[...]
    beta3 = jnp.concatenate([_lane_pick(pc, hg * hb + i).reshape(nc, CHUNK, 1) for i in range(hb)], axis=0)
    gc3 = jnp.concatenate([_lane_pick(pc, nh + hg * hb + i).reshape(nc, CHUNK, 1) for i in range(hb)], axis=0)
    gr3 = grow_ref[...].reshape(nb, 1, CHUNK)
    ii = lax.broadcasted_iota(jnp.int32, (nb, CHUNK, CHUNK), 1)
    jj = lax.broadcasted_iota(jnp.int32, (nb, CHUNK, CHUNK), 2)
    decay = jnp.exp(jnp.where(ii >= jj, gc3 - gr3, NEG_INF))
    kb = k3 * beta3
    m = jnp.where(ii > jj, _bdot_nt(kb, k3) * decay, 0.0)
    t = _unit_lower_inverse(m, ii, jj)
    sol = _bdot3(t, jnp.concatenate([v3 * beta3, kb * jnp.exp(gc3)], axis=2))
    u_s[...] = sol[:, :, :HEAD]
    w_s[...] = sol[:, :, HEAD:].astype(BF16)
    a_s[...] = jnp.where(ii >= jj, _bdot_nt(q3, k3) * decay, 0.0).astype(BF16)
    qg_s[...] = (q3 * jnp.exp(gc3)).astype(BF16)
    g_last = gc3[:, CHUNK - 1:CHUNK, :]
    kd_s[...] = (k3 * jnp.exp(g_last - gc3)).astype(BF16)
    egl = jnp.exp(g_last)

    for c in range(nc):
        for i in range(hb):
            idx = i * nc + c
            Sb = S[i]
            Sbf = Sb.astype(BF16)
            v_new = u_s[idx] - _dot(w_s[idx], Sbf)
            vnb = v_new.astype(BF16)
            o_s[idx] = _dot(qg_s[idx], Sbf) + _dot(a_s[idx], vnb)
            S[i] = Sb * egl[idx] + _dot_tn(kd_s[idx], vnb)

    nw = nw_ref[...]
    for i in range(hb):
        o = o_s[i * nc:(i + 1) * nc].reshape(R, HEAD)
        on = o * lax.rsqrt(jnp.mean(o * o, axis=-1, keepdims=True) + EPS) * nw
        o_ref[:, i * HEAD:(i + 1) * HEAD] = (on * _silu(z_ref[:, i * HEAD:(i + 1) * HEAD])).astype(o_ref.dtype)

    @pl.when(r == pl.num_programs(2) - 1)
    def _():
        s_ref[...] = S[...]


def _gdn_chunked(proj, conv_w, pcol, grow, norm_w, B, L, nh):
    M = B * L
    R = min(L, 512)
    nr = L // R
    nc = R // CHUNK
    hb = 4 if nh % 4 == 0 else nh
    ng = nh // hb
    W = hb * HEAD

    def col(off):
        return pl.BlockSpec((R, W), lambda b, g, r: (b * nr + r, off * ng + g))

    def cw(off):
        return pl.BlockSpec((CONV_W, W), lambda b, g, r: (0, off * ng + g))

    chunk_f32 = pltpu.VMEM((hb * nc, CHUNK, HEAD), F32)
    chunk_bf = pltpu.VMEM((hb * nc, CHUNK, HEAD), BF16)
    return pl.pallas_call(
        functools.partial(_gdn_chunk_kernel, nh=nh, hb=hb),
        grid=(B, ng, nr),
        in_specs=[col(0), col(1), col(2), col(3), cw(0), cw(1), cw(2),
                  pl.BlockSpec((R, HEAD), lambda b, g, r: (b * nr + r, 0)),
                  pl.BlockSpec((hb, nc, 1, CHUNK), lambda b, g, r: (g, b * nr + r, 0, 0)),
                  pl.BlockSpec((1, HEAD), lambda b, g, r: (0, 0))],
        out_specs=[pl.BlockSpec((R, W), lambda b, g, r: (b * nr + r, g)),
                   pl.BlockSpec((None, hb, HEAD, HEAD), lambda b, g, r: (b, g, 0, 0))],
        out_shape=[jax.ShapeDtypeStruct((M, nh * HEAD), BF16),
                   jax.ShapeDtypeStruct((B, nh, HEAD, HEAD), F32)],
        scratch_shapes=[pltpu.VMEM((R + 8, W), F32)] * 3 + [pltpu.VMEM((hb, HEAD, HEAD), F32),
                        chunk_f32, chunk_bf, chunk_bf, chunk_bf,
                        pltpu.VMEM((hb * nc, CHUNK, CHUNK), BF16), chunk_f32],
        compiler_params=_cparams(3, 48),
        name="gdn_chunked",
    )(proj, proj, proj, proj, conv_w, conv_w, conv_w, pcol, grow, norm_w)


def _gdn_step_kernel(q_ref, k_ref, v_ref, z_ref, cq_ref, ck_ref, cv_ref, cwq_ref, cwk_ref, cwv_ref,
                     g_ref, alog_ref, dtb_ref, nw_ref, s0_ref, o_ref, s_ref, *, nh):
    h = pl.program_id(1)

    def conv(c_ref, x_ref, cw_ref):
        cs, cw = c_ref[...], cw_ref[...]
        y = cs[0:1] * cw[0:1]
        for j in range(1, CONV_W - 1):
            y = y + cs[j:j + 1] * cw[j:j + 1]
        y = y + x_ref[...] * cw[CONV_W - 1:CONV_W]
        return _silu(y)

    def l2n(x):
        return x * lax.rsqrt(jnp.sum(x * x, axis=-1, keepdims=True) + EPS)

    q = l2n(conv(cq_ref, q_ref, cwq_ref)) * (HEAD ** -0.5)
    k = l2n(conv(ck_ref, k_ref, cwk_ref))
    v = conv(cv_ref, v_ref, cwv_ref)
    beta_all, g_all = _gdn_gates(g_ref[...], alog_ref[...], dtb_ref[...], nh)
    beta = _lane_pick(beta_all, h)
    g = _lane_pick(g_all, h)
    eg = jnp.exp(g)
    S0 = s0_ref[...]

    def rows8(x):
        return jnp.broadcast_to(x, (8, HEAD))

    kb = k * beta
    w = kb * eg
    v_new = v * beta - _dot_hi(rows8(w), S0)[0:1]
    a = jnp.sum(q * k, axis=-1, keepdims=True)
    o = _dot_hi(rows8(q * eg), S0)[0:1] + a * v_new
    s_ref[...] = S0 * eg + _dot_tn(rows8(k), rows8(v_new), precision=HI) * 0.125
    on = o * lax.rsqrt(jnp.mean(o * o, axis=-1, keepdims=True) + EPS) * nw_ref[...]
    o_ref[...] = on * _silu(z_ref[...])


def _gdn_step(proj, gates, conv_state, conv_w, alog, dtb, norm_w, s0, nh):
    B = proj.shape[0]
    proj3 = proj.reshape(B, 1, proj.shape[1])
    gates3 = gates.reshape(B, 1, HEAD)

    def col(off):
        return pl.BlockSpec((None, 1, HEAD), lambda b, h: (b, 0, off * nh + h))

    def cst(off):
        return pl.BlockSpec((None, CONV_W - 1, HEAD), lambda b, h: (b, 0, off * nh + h))

    def cw(off):
        return pl.BlockSpec((CONV_W, HEAD), lambda b, h: (0, off * nh + h))

    vec = pl.BlockSpec((1, nh), lambda b, h: (0, 0))
    o, s = pl.pallas_call(
        functools.partial(_gdn_step_kernel, nh=nh),
        grid=(B, nh),
        in_specs=[col(0), col(1), col(2), col(3), cst(0), cst(1), cst(2), cw(0), cw(1), cw(2),
                  pl.BlockSpec((None, 1, HEAD), lambda b, h: (b, 0, 0)), vec, vec,
                  pl.BlockSpec((1, HEAD), lambda b, h: (0, 0)),
                  pl.BlockSpec((None, None, HEAD, HEAD), lambda b, h: (b, h, 0, 0))],
        out_specs=[pl.BlockSpec((None, 1, HEAD), lambda b, h: (b, 0, h)),
                   pl.BlockSpec((None, None, HEAD, HEAD), lambda b, h: (b, h, 0, 0))],
        out_shape=[jax.ShapeDtypeStruct((B, 1, nh * HEAD), F32),
                   jax.ShapeDtypeStruct((B, nh, HEAD, HEAD), F32)],
        compiler_params=_cparams(2, 32),
        name="gdn_step",
    )(proj3, proj3, proj3, proj3, conv_state, conv_state, conv_state, conv_w, conv_w, conv_w,
      gates3, alog, dtb, norm_w, s0)
    return o.reshape(B, nh * HEAD), s


def _fcum_kernel(lf_ref, fcol_ref, frow_ref, carry, *, nh):
    R = lf_ref.shape[0]

    @pl.when(pl.program_id(1) == 0)
    def _():
        carry[...] = jnp.zeros_like(carry)

    ii = lax.broadcasted_iota(jnp.int32, (R, R), 0)
    jj = lax.broadcasted_iota(jnp.int32, (R, R), 1)
    tri = jnp.where(jj <= ii, 1.0, 0.0).astype(F32)
    f = _dot_hi(tri, lf_ref[...]) + carry[...]
    carry[...] = f[R - 1:R, :]
    fcol_ref[...] = f
    frow_ref[...] = f.T[0:nh, :]


def _fcum(logf, B, L, nh):
    R = 256
    nr = L // R
    return pl.pallas_call(
        functools.partial(_fcum_kernel, nh=nh),
        grid=(B, nr),
        in_specs=[pl.BlockSpec((R, HEAD), lambda b, r: (b * nr + r, 0))],
        out_specs=[pl.BlockSpec((R, HEAD), lambda b, r: (b * nr + r, 0)),
                   pl.BlockSpec((nh, R), lambda b, r: (0, b * nr + r))],
        out_shape=[jax.ShapeDtypeStruct((B * L, HEAD), F32), jax.ShapeDtypeStruct((nh, B * L), F32)],
        scratch_shapes=[pltpu.VMEM((1, HEAD), F32)],
        compiler_params=_cparams(2, 32),
        name="fcum",
    )(logf)


def _fox_prompt_kernel(q_ref, k_ref, v_ref, fcol_ref, frow_ref, o_ref, fr_s, m_s, l_s, acc_s):
    T = q_ref.shape[0]
    h = pl.program_id(1)
    qi = pl.program_id(2)
    scale = HEAD ** -0.5
    q = q_ref[...]
    fq = _lane_pick(fcol_ref[...], h)
    fr_s[...] = _row_pick(frow_ref[...], h)
    m_s[...] = jnp.full_like(m_s, NEG_INF)
    l_s[...] = jnp.zeros_like(l_s)
    acc_s[...] = jnp.zeros_like(acc_s)

    def block(ki, masked):
        off = pl.multiple_of(ki * T, T)
        kb = k_ref[pl.ds(off, T), :]
        vb = v_ref[pl.ds(off, T), :]
        s = _dot_nt(q, kb) * scale + fq - fr_s[:, pl.ds(off, T)]
        if masked:
            ii = lax.broadcasted_iota(jnp.int32, (T, T), 0)
            jj = lax.broadcasted_iota(jnp.int32, (T, T), 1)
            s = jnp.where(jj <= ii, s, NEG_INF)
        m_old = m_s[...]
        m_new = jnp.maximum(m_old, jnp.max(s, axis=-1, keepdims=True))
        alpha = jnp.exp(m_old - m_new)
        p = jnp.exp(s - m_new)
        l_s[...] = l_s[...] * alpha + jnp.sum(p, axis=-1, keepdims=True)
        acc_s[...] = acc_s[...] * alpha + _dot(p.astype(BF16), vb)
        m_s[...] = m_new

    def body(ki, c):
        block(ki, False)
        return c

    lax.fori_loop(0, qi, body, 0)
    block(qi, True)
    o_ref[...] = (acc_s[...] / l_s[...]).astype(o_ref.dtype)


def _fox_prompt(q, k, v, fcol, frow, B, L, nh):
    T = min(L, 512)
    nq = L // T
    return pl.pallas_call(
        _fox_prompt_kernel,
        grid=(B, nh, nq),
        in_specs=[pl.BlockSpec((T, HEAD), lambda b, h, i: (b * nq + i, h)),
                  pl.BlockSpec((L, HEAD), lambda b, h, i: (b, h)),
                  pl.BlockSpec((L, HEAD), lambda b, h, i: (b, h)),
                  pl.BlockSpec((T, HEAD), lambda b, h, i: (b * nq + i, 0)),
                  pl.BlockSpec((nh, L), lambda b, h, i: (0, b))],
        out_specs=pl.BlockSpec((T, HEAD), lambda b, h, i: (b * nq + i, h)),
        out_shape=jax.ShapeDtypeStruct((B * L, nh * HEAD), BF16),
        scratch_shapes=[pltpu.VMEM((1, L), F32), pltpu.VMEM((T, 1), F32), pltpu.VMEM((T, 1), F32),
                        pltpu.VMEM((T, HEAD), F32)],
        compiler_params=_cparams(3, 40),
        name="fox_prompt",
    )(q, k, v, fcol, frow)


def _decode_bias_kernel(pt_ref, *refs, pg):
    lf_refs = refs[:pg]
    lfn_ref, b_ref, carry = refs[pg:]
    P = lf_refs[0].shape[0]

    @pl.when(pl.program_id(1) == 0)
    def _():
        carry[...] = lfn_ref[...]

    ii = lax.broadcasted_iota(jnp.int32, (P, P), 0)
    jj = lax.broadcasted_iota(jnp.int32, (P, P), 1)
    upper = jnp.where(jj > ii, 1.0, 0.0).astype(F32)
    for s in range(pg):
        lf = lf_refs[s][...]
        b_ref[pg - 1 - s] = _dot_hi(upper, lf) + carry[...]
        carry[...] = carry[...] + jnp.sum(lf, axis=0, keepdims=True)


def _decode_bias(cache_logf, page_table, logf_new):
    n_phys, P, nh = cache_logf.shape
    B, n_pages = page_table.shape
    pg = 8 if n_pages % 8 == 0 else 1
    ng = n_pages // pg
    lfn = logf_new.reshape(B, 1, nh)

    def page(s):
        return pl.BlockSpec((None, P, nh), lambda b, g, pt: (pt[b, n_pages - 1 - (g * pg + s)], 0, 0))

    return pl.pallas_call(
        functools.partial(_decode_bias_kernel, pg=pg),
        grid_spec=pltpu.PrefetchScalarGridSpec(
            num_scalar_prefetch=1,
            grid=(B, ng),
            in_specs=[page(s) for s in range(pg)] + [pl.BlockSpec((None, 1, nh), lambda b, g, pt: (b, 0, 0))],
            out_specs=pl.BlockSpec((None, pg, P, nh), lambda b, g, pt: (b, ng - 1 - g, 0, 0)),
            scratch_shapes=[pltpu.VMEM((1, nh), F32)]),
        out_shape=jax.ShapeDtypeStruct((B, n_pages, P, nh), F32),
        compiler_params=_cparams(2, 32),
        name="decode_bias",
    )(page_table, *([cache_logf] * pg), lfn)


def _fox_decode_kernel(pt_ref, q_ref, *refs, nh, pg):
    k_refs, v_refs = refs[:pg], refs[pg:2 * pg]
    b_ref, kn_ref, vn_ref, o_ref, m_s, l_s, acc_s = refs[2 * pg:]
    p = pl.program_id(1)
    scale = HEAD ** -0.5

    @pl.when(p == 0)
    def _():
        m_s[...] = jnp.full_like(m_s, NEG_INF)
        l_s[...] = jnp.zeros_like(l_s)
        acc_s[...] = jnp.zeros_like(acc_s)

    q = q_ref[...]
    qb16 = q.astype(BF16)
    sub = lax.broadcasted_iota(jnp.int32, (nh, k_refs[0].shape[0]), 0)
    lane = lax.broadcasted_iota(jnp.int32, (nh, k_refs[0].shape[0]), 1)
    own = (lane % nh) == sub
    ss = [jnp.where(own, _dot_nt(qb16, k_refs[i][...].astype(BF16)) * scale + b_ref[i], NEG_INF) for i in range(pg)]
    m_old = m_s[...]
    m_new = m_old
    for s in ss:
        m_new = jnp.maximum(m_new, jnp.max(s, axis=-1, keepdims=True))
    alpha = jnp.exp(m_old - m_new)
    l_new = l_s[...] * alpha
    acc = acc_s[...] * alpha
    for i in range(pg):
        pr = jnp.exp(ss[i] - m_new)
        l_new = l_new + jnp.sum(pr, axis=-1, keepdims=True)
        acc = acc + _dot(pr.astype(BF16), v_refs[i][...].astype(BF16))
    l_s[...] = l_new
    acc_s[...] = acc
    m_s[...] = m_new

    @pl.when(p == pl.num_programs(1) - 1)
    def _():
        qb = q.astype(BF16).astype(F32)
        kn = kn_ref[...].astype(BF16).astype(F32)
        s_n = jnp.sum(qb * kn, axis=-1, keepdims=True) * scale
        m_f = jnp.maximum(m_s[...], s_n)
        a = jnp.exp(m_s[...] - m_f)
        pn = jnp.exp(s_n - m_f)
        l_f = l_s[...] * a + pn
        pnb = pn.astype(BF16).astype(F32)
        vn = vn_ref[...].astype(BF16).astype(F32)
        o_ref[...] = (acc_s[...] * a + pnb * vn) / l_f


def _fox_decode(q, cache_k, cache_v, bias, k_new, v_new, page_table):
    n_phys, P, nh, dh = cache_k.shape
    B, n_pages = page_table.shape
    k2 = cache_k.reshape(n_phys, P * nh, dh)
    v2 = cache_v.reshape(n_phys, P * nh, dh)
    b4 = bias.reshape(B, n_pages, 1, P * nh)
    pg = 4 if n_pages % 4 == 0 else 1
    head_spec = pl.BlockSpec((None, nh, dh), lambda b, p, pt: (b, 0, 0))

    def page(i):
        return pl.BlockSpec((None, P * nh, dh), lambda b, p, pt: (pt[b, p * pg + i], 0, 0))

    pages = [page(i) for i in range(pg)]
    return pl.pallas_call(
        functools.partial(_fox_decode_kernel, nh=nh, pg=pg),
        grid_spec=pltpu.PrefetchScalarGridSpec(
            num_scalar_prefetch=1,
            grid=(B, n_pages // pg),
            in_specs=[head_spec] + pages + pages +
                     [pl.BlockSpec((None, pg, 1, P * nh), lambda b, p, pt: (b, p, 0, 0)), head_spec, head_spec],
            out_specs=head_spec,
            scratch_shapes=[pltpu.VMEM((nh, 1), F32), pltpu.VMEM((nh, 1), F32), pltpu.VMEM((nh, dh), F32)]),
        out_shape=jax.ShapeDtypeStruct((B, nh, dh), F32),
        compiler_params=_cparams(2, 40),
        name="fox_decode",
    )(page_table, q, *([k2] * pg), *([v2] * pg), b4, k_new, v_new)


def _router_kernel(u_ref, w_ref, b_ref, comb_ref, info_ref, rank_ref, carry):
    tm = u_ref.shape[0]

    @pl.when(pl.program_id(0) == 0)
    def _():
        carry[...] = jnp.zeros_like(carry)

    lg = _dot(u_ref[...].astype(BF16), w_ref[...].astype(BF16)) + b_ref[...]
    lane = lax.broadcasted_iota(jnp.int32, lg.shape, 1)
    lg = jnp.where(lane < N_EXPERTS, lg, NEG_INF)
    m1 = jnp.max(lg, axis=-1, keepdims=True)
    i1 = jnp.min(jnp.where(lg == m1, lane, HEAD), axis=-1, keepdims=True)
    lg2 = jnp.where(lane == i1, NEG_INF, lg)
    m2 = jnp.max(lg2, axis=-1, keepdims=True)
    i2 = jnp.min(jnp.where(lg2 == m2, lane, HEAD), axis=-1, keepdims=True)
    e = jnp.exp(m2 - m1)
    w1 = 1.0 / (1.0 + e)
    w2 = e / (1.0 + e)
    comb_ref[...] = jnp.where(lane == i1, w1, 0.0) + jnp.where(lane == i2, w2, 0.0)
    sel = jnp.where((lane == i1) | (lane == i2), 1.0, 0.0)
    ii = lax.broadcasted_iota(jnp.int32, (tm, tm), 0)
    jj = lax.broadcasted_iota(jnp.int32, (tm, tm), 1)
    before = jnp.where(jj < ii, 1.0, 0.0).astype(BF16)
    rank = _dot(before, sel.astype(BF16)) + carry[...]
    carry[...] = carry[...] + jnp.sum(sel, axis=0, keepdims=True)
    rank_ref[...] = jnp.broadcast_to(carry[...], rank_ref.shape)
    r1 = jnp.sum(jnp.where(lane == i1, rank, 0.0), axis=-1, keepdims=True)
    r2 = jnp.sum(jnp.where(lane == i2, rank, 0.0), axis=-1, keepdims=True)
    info = jnp.where(lane == 0, i1.astype(F32), 0.0) + jnp.where(lane == 1, i2.astype(F32), 0.0)
    info = info + jnp.where(lane == 2, w1, 0.0) + jnp.where(lane == 3, w2, 0.0)
    info_ref[...] = info + jnp.where(lane == 4, r1, 0.0) + jnp.where(lane == 5, r2, 0.0)


def _router(u, w_router, b_router):
    M, D = u.shape
    tm = min(M, 512)
    wp = jnp.pad(w_router, ((0, 0), (0, HEAD - N_EXPERTS)))
    bp = jnp.pad(b_router, (0, HEAD - N_EXPERTS)).reshape(1, HEAD)
    return pl.pallas_call(
        _router_kernel,
        grid=(M // tm,),
        in_specs=[pl.BlockSpec((tm, D), lambda i: (i, 0)),
                  pl.BlockSpec((D, HEAD), lambda i: (0, 0)),
                  pl.BlockSpec((1, HEAD), lambda i: (0, 0))],
        out_specs=[pl.BlockSpec((tm, HEAD), lambda i: (i, 0)),
                   pl.BlockSpec((tm, HEAD), lambda i: (i, 0)),
                   pl.BlockSpec((None, 8, HEAD), lambda i: (i, 0, 0))],
        out_shape=[jax.ShapeDtypeStruct((M, HEAD), F32), jax.ShapeDtypeStruct((M, HEAD), F32),
                   jax.ShapeDtypeStruct((M // tm, 8, HEAD), F32)],
        scratch_shapes=[pltpu.VMEM((1, HEAD), F32)],
        compiler_params=_cparams(1, 32),
        name="router",
    )(u, wp, bp)


MOE_TM = 512


def _moe_plan(info, counts, M):
    tm = MOE_TM
    cnt = counts[-1, 0, :N_EXPERTS].astype(jnp.int32)
    padded = ((cnt + tm - 1) // tm) * tm
    ends = jnp.cumsum(padded)
    off = ends - padded
    e1 = info[:, 0].astype(jnp.int32)
    e2 = info[:, 1].astype(jnp.int32)
    d1 = off[e1] + info[:, 4].astype(jnp.int32)
    d2 = off[e2] + info[:, 5].astype(jnp.int32)
    n_rows = 2 * M + N_EXPERTS * tm
    nt = n_rows // tm
    tok = jnp.arange(M, dtype=jnp.int32)
    src = jnp.zeros((n_rows,), jnp.int32).at[d1].set(tok).at[d2].set(tok)
    start = jnp.arange(nt, dtype=jnp.int32) * tm
    te_raw = jnp.sum((start[:, None] >= ends[None, :]).astype(jnp.int32), axis=1)
    valid = (te_raw < N_EXPERTS).astype(jnp.int32)
    last_e = jnp.max(jnp.where(cnt > 0, jnp.arange(N_EXPERTS, dtype=jnp.int32), 0))
    te = jnp.where(valid == 1, te_raw, last_e).astype(jnp.int32)
    first = jnp.concatenate([jnp.ones((1,), jnp.int32), (te[1:] != te[:-1]).astype(jnp.int32)])
    return dict(d1=d1, d2=d2, src=src.reshape(nt, 1, tm), te=te, first=first, valid=valid, n_rows=n_rows, nt=nt)


def _moe_gather_kernel(valid_ref, src_ref, u_hbm, o_ref, buf, sem):
    tm = o_ref.shape[0]
    i = pl.program_id(0)

    @pl.when(valid_ref[i] == 1)
    def _():
        def issue(r, c):
            pltpu.make_async_copy(u_hbm.at[pl.ds(src_ref[0, r], 1)], buf.at[pl.ds(r, 1)], sem).start()
            return c

        lax.fori_loop(0, tm, issue, 0)
        pltpu.make_async_copy(u_hbm.at[pl.ds(0, tm)], buf, sem).wait()
        o_ref[...] = buf[...].astype(o_ref.dtype)

    @pl.when(valid_ref[i] == 0)
    def _():
        o_ref[...] = jnp.zeros_like(o_ref)


def _moe_gather(u, plan):
    M, D = u.shape
    tm, nt = MOE_TM, plan['nt']
    return pl.pallas_call(
        _moe_gather_kernel,
        grid_spec=pltpu.PrefetchScalarGridSpec(
            num_scalar_prefetch=1,
            grid=(nt,),
            in_specs=[pl.BlockSpec((None, 1, tm), lambda i, va: (i, 0, 0), memory_space=pltpu.SMEM),
                      pl.BlockSpec(memory_space=pl.ANY)],
            out_specs=pl.BlockSpec((tm, D), lambda i, va: (i, 0)),
            scratch_shapes=[pltpu.VMEM((tm, D), F32), pltpu.SemaphoreType.DMA(())]),
        out_shape=jax.ShapeDtypeStruct((plan['n_rows'], D), BF16),
        compiler_params=_cparams(1, 32),
        name="moe_gather",
    )(plan['valid'], plan['src'], u)


def _grouped_matmul(x, ws, plan, *, tn, n_blocks, epilogue, outs, name, vmem_mb=48):
    Mp, K = x.shape
    tm = MOE_TM
    nw, no = len(ws), len(outs)

    def kern(te_ref, first_ref, valid_ref, *refs):
        x_ref = refs[0]
        w_refs = refs[1:1 + nw]
        o_refs = refs[1 + nw:1 + nw + no]
        wbf = refs[1 + nw + no:]
        i = pl.program_id(1)

        @pl.when(first_ref[i] == 1)
        def _():
            for wr, wb in zip(w_refs, wbf):
                wb[...] = wr[...].astype(BF16)

        @pl.when(valid_ref[i] == 1)
        def _():
            xv = x_ref[...]
            accs = [_dot(xv, wb[...]) for wb in wbf]
            for o_ref, o in zip(o_refs, epilogue(accs)):
                o_ref[...] = o.astype(o_ref.dtype)

        @pl.when(valid_ref[i] == 0)
        def _():
            for o_ref in o_refs:
                o_ref[...] = jnp.zeros_like(o_ref)

    in_specs = [pl.BlockSpec((tm, K), lambda j, i, te, fi, va: (i, 0))]
    for arr, lead, off in ws:
        nl = len(lead) + 1
        in_specs.append(pl.BlockSpec(
            (None,) * nl + (K, tn),
            functools.partial(lambda j, i, te, fi, va, lead, off: lead + (te[i], 0, off + j), lead=tuple(lead), off=off)))
    return pl.pallas_call(
        kern,
        grid_spec=pltpu.PrefetchScalarGridSpec(
            num_scalar_prefetch=3,
            grid=(n_blocks, Mp // tm),
            in_specs=in_specs,
            out_specs=[pl.BlockSpec((tm, tn), lambda j, i, te, fi, va: (i, j)) for _ in outs],
            scratch_shapes=[pltpu.VMEM((K, tn), BF16) for _ in ws]),
        out_shape=outs,
        compiler_params=_cparams(2, vmem_mb),
        name=name,
    )(plan['te'], plan['first'], plan['valid'], x, *[arr for arr, _, _ in ws])


def _moe_combine_kernel(d1_ref, d2_ref, info_ref, h_ref, gate_ref, y_hbm, hn_ref, buf, sem):
    tl = h_ref.shape[1]

    def issue(r, c):
        pltpu.make_async_copy(y_hbm.at[pl.ds(d1_ref[0, r], 1)], buf.at[0, pl.ds(r, 1)], sem).start()
        pltpu.make_async_copy(y_hbm.at[pl.ds(d2_ref[0, r], 1)], buf.at[1, pl.ds(r, 1)], sem).start()
        return c

    lax.fori_loop(0, tl, issue, 0)
    pltpu.make_async_copy(y_hbm.at[pl.ds(0, tl)], buf.at[0], sem).wait()
    pltpu.make_async_copy(y_hbm.at[pl.ds(0, tl)], buf.at[1], sem).wait()
    info = info_ref[...]
    y = info[:, 2:3] * buf[0] + info[:, 3:4] * buf[1]
    hn_ref[0] = h_ref[0] + gate_ref[0] * y


def _moe_combine(y_sorted, plan, info, h, gate):
    B, L, D = h.shape
    tl = 256
    nl = L // tl
    d1 = plan['d1'].reshape(B * nl, 1, tl)
    d2 = plan['d2'].reshape(B * nl, 1, tl)
    smem_rows = pl.BlockSpec((None, 1, tl), lambda b, t: (b * nl + t, 0, 0), memory_space=pltpu.SMEM)
    return pl.pallas_call(
        _moe_combine_kernel,
        grid=(B, nl),
        in_specs=[smem_rows, smem_rows,
                  pl.BlockSpec((tl, HEAD), lambda b, t: (b * nl + t, 0)),
                  pl.BlockSpec((1, tl, D), lambda b, t: (b, t, 0)),
                  pl.BlockSpec((1, 1, D), lambda b, t: (b, 0, 0)),
                  pl.BlockSpec(memory_space=pl.ANY)],
        out_specs=pl.BlockSpec((1, tl, D), lambda b, t: (b, t, 0)),
        out_shape=jax.ShapeDtypeStruct((B, L, D), F32),
        scratch_shapes=[pltpu.VMEM((2, tl, D), F32), pltpu.SemaphoreType.DMA(())],
        compiler_params=_cparams(2, 32),
        name="moe_combine",
    )(d1, d2, info, h, gate, y_sorted)


def _moe_sparse(u, h, gate, w_router, b_router, w_gu, w_down, j):
    M, D = u.shape
    d_ffe = w_down.shape[2]
    comb, info, counts = _router(u, w_router, b_router)
    plan = _moe_plan(info, counts, M)
    xs = _moe_gather(u, plan)
    tn = min(512, _ffn_tn(d_ffe))
    nb = d_ffe // tn
    (act,) = _grouped_matmul(xs, [(w_gu, (j,), 0), (w_gu, (j,), nb)], plan, tn=tn, n_blocks=nb,
                             epilogue=lambda accs: (_silu(accs[0]) * accs[1],),
                             outs=[jax.ShapeDtypeStruct((plan['n_rows'], d_ffe), BF16)], name="moe_gu")
    tn_d = 512 if D % 512 == 0 else D
    (ys,) = _grouped_matmul(act, [(w_down, (j,), 0)], plan, tn=tn_d, n_blocks=D // tn_d,
                            epilogue=lambda accs: (accs[0],),
                            outs=[jax.ShapeDtypeStruct((plan['n_rows'], D), F32)], name="moe_down")
    return _moe_combine(ys, plan, info, h, gate)


def _ffn_tn(d_ff):
    for tn in (1024, 512, 256, 128):
        if d_ff % tn == 0:
            return tn
    raise ValueError(d_ff)


def _trunk(x, mods, mod_kv, p, *, prompt, conv_bufs=None, gdn_states=None,
           cache_k=None, cache_v=None, cache_logf=None, page_table=None):
    B, L, D = x.shape
    M = B * L
    depth = p['w_ada'].shape[0]
    n_a = p['gdn_w_in'].shape[0]
    nh = D // HEAD
    d_ff = p['ffn_w_down'].shape[1]
    d_ffe = p['moe_w_down'].shape[2]
    tn_d = 1024 if D % 1024 == 0 else D

    if prompt:
        def mod3(v):
            return v.reshape(B, 1, D)
        h = x
    else:
        def mod3(v):
            return v.reshape(1, B, D)
        h = x.reshape(1, B, D)

    def rows(a):
        return a.reshape(M, a.shape[-1])

    def as_h(a):
        return a.reshape(h.shape)

    def gain(v):
        return v.reshape(1, -1)

    new_conv, new_gdn = [], []
    k_new = v_new = logf_new = None
    y = gate = None
    for layer in range(depth):
        sh_m, sc_m, g_m, sh_f, sc_f, g_f = [mod3(mods[layer][:, i * D:(i + 1) * D]) for i in range(6)]
        if y is None:
            u = _resmod(h, gain=gain(p['norm_mix'][layer]), shift=sh_m, scale=sc_m)
        else:
            h, u = _resmod(h, y, gate, gain=gain(p['norm_mix'][layer]), shift=sh_m, scale=sc_m)
        u2 = rows(u)
        if layer < n_a:
            w_in = p['gdn_w_in']
            proj = _mm_plain(u2, w_in, (layer,), n_cols=4 * nh * HEAD, tn=tn_d)
            gates = _mm_plain(u2, w_in, (layer,), n_cols=HEAD, col0=4 * nh, tn=HEAD, mask_cols=2 * nh)
            alog = p['gdn_a_log'][layer].reshape(1, nh)
            dtb = p['gdn_dt_bias'][layer].reshape(1, nh)
            nw = p['gdn_norm'][layer].reshape(1, HEAD)
            cw = p['gdn_conv_w'][layer]
            n_conv = 3 * nh * HEAD
            if prompt:
                pcol, grow = _gdn_prep(gates, alog, dtb, nh)
                o, s_new = _gdn_chunked(proj, cw, pcol, grow, nw, B, L, nh)
                new_conv.append(proj.reshape(B, L, -1)[:, L - (CONV_W - 1):, :n_conv])
            else:
                o, s_new = _gdn_step(proj, gates, conv_bufs[layer], cw, alog, dtb, nw, gdn_states[layer], nh)
                new_conv.append(jnp.concatenate([conv_bufs[layer][:, 1:], proj[:, None, :n_conv]], axis=1))
            new_gdn.append(s_new)
            y = _mm_plain(o, p['gdn_w_out'], (layer,), n_cols=D, tn=tn_d)
        else:
            b = layer - n_a
            qn = gain(p['fox_q_norm'][b])
            if prompt:
                (q,) = _mm_headnorm(u2, p['fox_w_q'], (b,), qn, n_cols=D, tn=tn_d, out_dtypes=(BF16,))
                o = _fox_prompt(q, kbf, vbf, fcol, frow, B, L, nh)
            else:
                (q,) = _mm_headnorm(u2, p['fox_w_q'], (b,), qn, n_cols=D, tn=tn_d, out_dtypes=(F32,))
                o = _fox_decode(q.reshape(B, nh, HEAD), cache_k, cache_v, dec_bias,
                                k_new.reshape(B, nh, HEAD), v_new.reshape(B, nh, HEAD), page_table)
                o = o.reshape(B, nh * HEAD)
            y = _mm_plain(o, p['fox_w_o'], (b,), n_cols=D, tn=tn_d)
        sparse = prompt and layer % 2 == 1
        h, u = _resmod(h, as_h(y), g_m, gain=gain(p['norm_ffn'][layer]), shift=sh_f, scale=sc_f,
                       u_dtype=F32 if sparse else BF16)
        u2 = rows(u)
        j = layer // 2
        if sparse:
            h = _moe_sparse(u2, h, g_f, p['moe_w_router'][j], p['moe_b_router'][j], p['moe_w_gu'], p['moe_w_down'], j)
            y = None
        elif layer % 2 == 0:
            act = _mm_swiglu(u2, p['ffn_w_gu'], (j,), d_ff=d_ff, tn=min(512, _ffn_tn(d_ff)))
            y = _mm_plain(act, p['ffn_w_down'], (j,), n_cols=D, tn=512 if D % 512 == 0 else D)
        else:
            comb, _, _ = _router(u2, p['moe_w_router'][j], p['moe_b_router'][j])
            y = None
            for e in range(N_EXPERTS):
                act = _mm_swiglu(u2, p['moe_w_gu'], (j, e), d_ff=d_ffe, tn=min(512, _ffn_tn(d_ffe)))
                y = _mm_moe_down(act, p['moe_w_down'], (j, e), comb, e, y, tn=512 if D % 512 == 0 else D)
        gate = g_f
        if y is not None:
            y = as_h(y)
        if layer == n_a - 1:
            if y is not None:
                h = _resmod(h, y, gate)
                y = None
            shift, scale = mod3(mod_kv[:, :D]), mod3(mod_kv[:, D:])
            ukv = rows(_resmod(h, gain=gain(p['kv_norm']), shift=shift, scale=scale))
            w_kvf = p['w_kvf']
            k_new, kbf = _mm_headnorm(ukv, w_kvf, (), gain(p['k_norm']), n_cols=D, tn=tn_d, out_dtypes=(F32, BF16))
            v_new, vbf = _matmul(ukv, [(w_kvf, (), D // tn_d)], tm=_pick_tm(M, D), tn=tn_d, n_blocks=D // tn_d,
                                 epilogue=lambda accs, ev: (accs[0], accs[0]),
                                 outs=[(jax.ShapeDtypeStruct((M, D), dt), (_pick_tm(M, D), tn_d), lambda j, i: (i, j))
                                       for dt in (F32, BF16)])
            b_f = jnp.pad(p['b_f'], (0, HEAD - nh)).reshape(1, HEAD)
            logf_pad = _log_forget(ukv, w_kvf, b_f, col0=2 * D // HEAD, nh=nh)
            logf_new = logf_pad[:, :nh]
            if prompt:
                fcol, frow = _fcum(logf_pad, B, L, nh)
            else:
                dec_bias = _decode_bias(cache_logf, page_table, logf_new)
    if y is not None:
        h = _resmod(h, y, gate)
    return (h.reshape(B, L, D), k_new.reshape(B, L, nh, HEAD), v_new.reshape(B, L, nh, HEAD),
            logf_new.reshape(B, L, nh), jnp.stack(new_gdn), jnp.stack(new_conv))


def _log_forget(ukv, w_kvf, b_f, *, col0, nh):
    M, K = ukv.shape
    tm = _pick_tm(M, K)

    def epi(accs, ev):
        lane = lax.broadcasted_iota(jnp.int32, accs[0].shape, 1)
        return (jnp.where(lane < nh, -_softplus(-(accs[0] + ev[0])), 0.0),)

    (out,) = _matmul(ukv, [(w_kvf, (), col0)], tm=tm, tn=HEAD, n_blocks=1, epilogue=epi,
                     outs=[(jax.ShapeDtypeStruct((M, HEAD), F32), (tm, HEAD), lambda j, i: (i, 0))],
                     extras=[(b_f, (1, HEAD), lambda j, i: (0, 0))])
    return out


def kernel(x_prompt, x_sample, state_gdn, state_conv, cache_k, cache_v, cache_logf, page_table, c_prompt, c_sample, w_ada, b_ada, norm_mix, norm_ffn, gdn_w_in, gdn_conv_w, gdn_a_log, gdn_dt_bias, gdn_norm, gdn_w_out, kv_norm, w_ada_kv, b_ada_kv, w_kvf, b_f, k_norm, fox_w_q, fox_q_norm, fox_w_o, ffn_w_gu, ffn_w_down, moe_w_router, moe_b_router, moe_w_gu, moe_w_down):
    p = dict(w_ada=w_ada, b_ada=b_ada, norm_mix=norm_mix, norm_ffn=norm_ffn, gdn_w_in=gdn_w_in,
             gdn_conv_w=gdn_conv_w, gdn_a_log=gdn_a_log, gdn_dt_bias=gdn_dt_bias, gdn_norm=gdn_norm,
             gdn_w_out=gdn_w_out, kv_norm=kv_norm, w_ada_kv=w_ada_kv, b_ada_kv=b_ada_kv, w_kvf=w_kvf,
             b_f=b_f, k_norm=k_norm, fox_w_q=fox_w_q, fox_q_norm=fox_q_norm, fox_w_o=fox_w_o,
             ffn_w_gu=ffn_w_gu, ffn_w_down=ffn_w_down, moe_w_router=moe_w_router,
             moe_b_router=moe_b_router, moe_w_gu=moe_w_gu, moe_w_down=moe_w_down)
    bp, bs = c_prompt.shape[0], c_sample.shape[0]
    D = c_prompt.shape[1]
    depth = w_ada.shape[0]
    n_rows = -(-(bp + bs) // 8) * 8
    c_all = jnp.pad(jnp.concatenate([c_prompt, c_sample], axis=0), ((0, n_rows - bp - bs), (0, 0)))
    tn_a = 1024 if D % 1024 == 0 else D
    mods = [_mm_plain(c_all, w_ada, (l,), n_cols=6 * D, tn=tn_a, bias=b_ada[l].reshape(1, -1), silu_x=True)
            for l in range(depth)]
    mods_p = [m[:bp] for m in mods]
    mods_s = [m[bp:bp + bs] for m in mods]
    mod_kv = _mm_plain(c_all, w_ada_kv, (), n_cols=2 * D, tn=tn_a, bias=b_ada_kv.reshape(1, -1), silu_x=True)

    y_p, k_p, v_p, f_p, s_p, cv_p = _trunk(x_prompt, mods_p, mod_kv[:bp], p, prompt=True)
    y_s, k_s, v_s, f_s, s_s, cv_s = _trunk(x_sample, mods_s, mod_kv[bp:bp + bs], p, prompt=False,
                                           conv_bufs=state_conv, gdn_states=state_gdn, cache_k=cache_k,
                                           cache_v=cache_v, cache_logf=cache_logf, page_table=page_table)
    return (y_p, y_s, k_p, v_p, f_p, s_p, cv_p, k_s, v_s, f_s, s_s, cv_s)
```

```python
import functools
import math

import jax
import jax.numpy as jnp
from jax import lax
from jax.experimental import pallas as pl
from jax.experimental.pallas import tpu as pltpu

F32 = jnp.float32
BF16 = jnp.bfloat16
EPS = 1e-6
HEAD = 128
CHUNK = 64
CONV_W = 4
N_EXPERTS = 8
HI = lax.Precision.HIGHEST
NEG_INF = float("-inf")


def _cparams(n_axes, vmem_mb):
    return pltpu.CompilerParams(dimension_semantics=("arbitrary",) * n_axes,
                                vmem_limit_bytes=vmem_mb << 20)


def _sigmoid(x):
    return 1.0 / (1.0 + jnp.exp(-x))


def _silu(x):
    return x * _sigmoid(x)


def _softplus(x):
    return jnp.maximum(x, 0.0) + jnp.log1p(jnp.exp(-jnp.abs(x)))


def _dot(a, b):
    return jnp.dot(a, b, preferred_element_type=F32)


def _dot_nt(a, b):
    return lax.dot_general(a, b, (((1,), (1,)), ((), ())), preferred_element_type=F32)


def _dot_tn(a, b, precision=None):
    return lax.dot_general(a, b, (((0,), (0,)), ((), ())), preferred_element_type=F32, precision=precision)


def _dot_hi(a, b):
    return jnp.dot(a, b, preferred_element_type=F32, precision=HI)


SPLIT_MAX_ROWS = 64


def _split(a):
    hi = a.astype(BF16)
    return hi, (a - hi.astype(F32)).astype(BF16)


def _lane_pick(x, idx):
    lane = lax.broadcasted_iota(jnp.int32, x.shape, 1)
    return jnp.sum(jnp.where(lane == idx, x, 0.0), axis=1, keepdims=True)


def _row_pick(x, idx):
    sub = lax.broadcasted_iota(jnp.int32, x.shape, 0)
    return jnp.sum(jnp.where(sub == idx, x, 0.0), axis=0, keepdims=True)


def _matmul(x, ws, *, tm, tn, n_blocks, epilogue, outs, extras=(), silu_x=False, vmem_mb=48):
    M, K = x.shape
    nw, ne, no = len(ws), len(extras), len(outs)
    split = x.dtype == F32 and M <= SPLIT_MAX_ROWS

    def kern(*refs):
        x_ref = refs[0]
        w_refs = refs[1:1 + nw]
        e_refs = refs[1 + nw:1 + nw + ne]
        o_refs = refs[1 + nw + ne:1 + nw + ne + no]
        wbf = refs[1 + nw + ne + no:1 + nw + ne + no + nw]
        wlo = refs[1 + nw + ne + no + nw:]

        @pl.when(pl.program_id(1) == 0)
        def _():
            for n, (wr, wb) in enumerate(zip(w_refs, wbf)):
                if split:
                    wb[...], wlo[n][...] = _split(wr[...])
                else:
                    wb[...] = wr[...].astype(BF16)

        xv = x_ref[...]
        if silu_x:
            xv = _silu(xv.astype(F32))
        if split:
            xh, xl = _split(xv)
            accs = [_dot(xh, wb[...]) + (_dot(xh, wl[...]) + _dot(xl, wb[...])) for wb, wl in zip(wbf, wlo)]
        else:
            xv = xv.astype(BF16)
            accs = [_dot(xv, wb[...]) for wb in wbf]
        res = epilogue(accs, [e[...] for e in e_refs])
        for o_ref, o in zip(o_refs, res):
            o_ref[...] = o.astype(o_ref.dtype)

    in_specs = [pl.BlockSpec((tm, K), lambda j, i: (i, 0))]
    args = [x]
    for arr, lead, off in ws:
        nl = len(lead)
        in_specs.append(pl.BlockSpec((None,) * nl + (K, tn),
                                     functools.partial(lambda j, i, lead, off: lead + (0, off + j), lead=tuple(lead), off=off)))
        args.append(arr)
    for arr, bs, im in extras:
        in_specs.append(pl.BlockSpec(bs, im))
        args.append(arr)
    res = pl.pallas_call(
        kern,
        grid=(n_blocks, M // tm),
        in_specs=in_specs,
        out_specs=[pl.BlockSpec(bs, im) for _, bs, im in outs],
        out_shape=[sd for sd, _, _ in outs],
        scratch_shapes=[pltpu.VMEM((K, tn), BF16) for _ in range(nw * (2 if split else 1))],
        compiler_params=_cparams(2, vmem_mb),
        name=f"mm_m{M}_k{K}_n{tn}x{n_blocks}_w{nw}",
    )(*args)
    return res


def _pick_tm(M, K):
    if M <= 1024:
        return M
    return 1024 if K <= 2048 else 512


def _mm_plain(x, w, lead, *, n_cols, col0=0, tn, out_dtype=F32, bias=None, silu_x=False, mask_cols=None):
    M, K = x.shape
    tm = _pick_tm(M, K)
    nb = n_cols // tn
    extras = []
    if bias is not None:
        extras.append((bias, (1, tn), lambda j, i: (0, j)))

    def epi(accs, ev):
        y = accs[0]
        if bias is not None:
            y = y + ev[0]
        if mask_cols is not None:
            lane = lax.broadcasted_iota(jnp.int32, y.shape, 1)
            y = jnp.where(lane < mask_cols, y, 0.0)
        return (y,)

    (out,) = _matmul(x, [(w, lead, col0)], tm=tm, tn=tn, n_blocks=nb, epilogue=epi, silu_x=silu_x,
                     outs=[(jax.ShapeDtypeStruct((M, n_cols), out_dtype), (tm, tn), lambda j, i: (i, j))],
                     extras=extras)
    return out


def _head_rmsnorm(y, gain):
    parts = []
    for hh in range(y.shape[1] // HEAD):
        yh = y[:, hh * HEAD:(hh + 1) * HEAD]
        parts.append(yh * lax.rsqrt(jnp.mean(yh * yh, axis=-1, keepdims=True) + EPS) * gain)
    return jnp.concatenate(parts, axis=1) if len(parts) > 1 else parts[0]


def _mm_headnorm(x, w, lead, gain, *, n_cols, tn, out_dtypes):
    M, K = x.shape
    tm = _pick_tm(M, K)

    def epi(accs, ev):
        y = _head_rmsnorm(accs[0], ev[0])
        return tuple(y for _ in out_dtypes)

    return _matmul(x, [(w, lead, 0)], tm=tm, tn=tn, n_blocks=n_cols // tn, epilogue=epi,
                   outs=[(jax.ShapeDtypeStruct((M, n_cols), dt), (tm, tn), lambda j, i: (i, j)) for dt in out_dtypes],
                   extras=[(gain, (1, HEAD), lambda j, i: (0, 0))])


def _mm_swiglu(x, w, lead, *, d_ff, tn):
    M, K = x.shape
    tm = _pick_tm(M, K)
    nb = d_ff // tn

    def epi(accs, ev):
        return (_silu(accs[0]) * accs[1],)

    (out,) = _matmul(x, [(w, lead, 0), (w, lead, nb)], tm=tm, tn=tn, n_blocks=nb, epilogue=epi,
                     outs=[(jax.ShapeDtypeStruct((M, d_ff), x.dtype), (tm, tn), lambda j, i: (i, j))])
    return out


def _mm_moe_down(act, w, lead, comb, e, prev, *, tn):
    M, K = act.shape
    N = w.shape[-1]
    tm = _pick_tm(M, K)
    extras = [(comb, (tm, HEAD), lambda j, i: (i, 0))]
    if prev is not None:
        extras.append((prev, (tm, tn), lambda j, i: (i, j)))

    def epi(accs, ev):
        y = ev[0][:, e:e + 1] * accs[0]
        if prev is not None:
            y = ev[1] + y
        return (y,)

    (out,) = _matmul(act, [(w, lead, 0)], tm=tm, tn=tn, n_blocks=N // tn, epilogue=epi,
                     outs=[(jax.ShapeDtypeStruct((M, N), F32), (tm, tn), lambda j, i: (i, j))],
                     extras=extras)
    return out


def _resmod_kernel(*refs, has_y, has_mod):
    it = iter(refs)
    h_ref = next(it)
    if has_y:
        y_ref = next(it)
        gate_ref = next(it)
    if has_mod:
        gain_ref = next(it)
        shift_ref = next(it)
        scale_ref = next(it)
    if has_y:
        hn_ref = next(it)
    if has_mod:
        u_ref = next(it)
    h = h_ref[0]
    if has_y:
        h = h + gate_ref[0] * y_ref[0].astype(F32)
        hn_ref[0] = h
    if has_mod:
        ms = jnp.mean(h * h, axis=-1, keepdims=True)
        y = h * lax.rsqrt(ms + EPS) * gain_ref[...]
        u_ref[0] = (y * (1.0 + scale_ref[0]) + shift_ref[0]).astype(u_ref.dtype)


def _resmod(h, y=None, gate=None, gain=None, shift=None, scale=None, u_dtype=None):
    B, L, D = h.shape
    tl = min(L, 256)
    has_y, has_mod = y is not None, gain is not None

    def mod_spec(a):
        if a.shape[1] == 1:
            return pl.BlockSpec((1, 1, D), lambda b, t: (b, 0, 0))
        return pl.BlockSpec((1, tl, D), lambda b, t: (b, t, 0))

    row_spec = pl.BlockSpec((1, tl, D), lambda b, t: (b, t, 0))
    args, in_specs, out_shape, out_specs = [h], [row_spec], [], []
    if has_y:
        args += [y, gate]
        in_specs += [row_spec, mod_spec(gate)]
        out_shape.append(jax.ShapeDtypeStruct((B, L, D), F32))
        out_specs.append(row_spec)
    if has_mod:
        args += [gain, shift, scale]
        in_specs += [pl.BlockSpec((1, D), lambda b, t: (0, 0)), mod_spec(shift), mod_spec(scale)]
        out_shape.append(jax.ShapeDtypeStruct((B, L, D), u_dtype or BF16))
        out_specs.append(row_spec)
    res = pl.pallas_call(
        functools.partial(_resmod_kernel, has_y=has_y, has_mod=has_mod),
        grid=(B, L // tl), in_specs=in_specs, out_specs=out_specs, out_shape=out_shape,
        compiler_params=_cparams(2, 40),
        name="resmod",
    )(*args)
    return res if len(res) > 1 else res[0]


def _gdn_gates(gt, alog, dtb, nh):
    beta = _sigmoid(gt[:, 0:nh])
    g = -jnp.exp(alog) * _softplus(gt[:, nh:2 * nh] + dtb)
    return beta, g


def _gdn_prep_kernel(g_ref, alog_ref, dtb_ref, pcol_ref, grow_ref, *, nh):
    R = g_ref.shape[0]
    beta, g = _gdn_gates(g_ref[...], alog_ref[...], dtb_ref[...], nh)
    ii = lax.broadcasted_iota(jnp.int32, (R, R), 0)
    jj = lax.broadcasted_iota(jnp.int32, (R, R), 1)
    tri = jnp.where((jj <= ii) & ((ii // CHUNK) == (jj // CHUNK)), 1.0, 0.0).astype(F32)
    gc = _dot_hi(tri, g)
    pc = jnp.concatenate([beta, gc, jnp.zeros((R, HEAD - 2 * nh), F32)], axis=1)
    pcol_ref[...] = pc
    pct = pc.T
    for hh in range(nh):
        for c in range(R // CHUNK):
            grow_ref[hh, c] = pct[nh + hh:nh + hh + 1, c * CHUNK:(c + 1) * CHUNK]


def _gdn_prep(gates, alog, dtb, nh):
    M = gates.shape[0]
    R = 256
    return pl.pallas_call(
        functools.partial(_gdn_prep_kernel, nh=nh),
        grid=(M // R,),
        in_specs=[pl.BlockSpec((R, HEAD), lambda r: (r, 0)),
                  pl.BlockSpec((1, nh), lambda r: (0, 0)),
                  pl.BlockSpec((1, nh), lambda r: (0, 0))],
        out_specs=[pl.BlockSpec((R, HEAD), lambda r: (r, 0)),
                   pl.BlockSpec((nh, R // CHUNK, 1, CHUNK), lambda r: (0, r, 0, 0))],
        out_shape=[jax.ShapeDtypeStruct((M, HEAD), F32), jax.ShapeDtypeStruct((nh, M // CHUNK, 1, CHUNK), F32)],
        compiler_params=_cparams(1, 32),
        name="gdn_prep",
    )(gates, alog, dtb)


def _bdot(a, b):
    return jnp.einsum('cij,cjk->cik', a.astype(BF16), b.astype(BF16), preferred_element_type=F32)


def _bdot_nt(a, b):
    return jnp.einsum('cik,cjk->cij', a.astype(BF16), b.astype(BF16), preferred_element_type=F32)


def _unit_lower_inverse(m, ii, jj):
    bi, bj = ii // 16, jj // 16
    md = jnp.where(bi == bj, m, 0.0)
    t = jnp.where(ii == jj, 1.0, 0.0) - md
    xp = _bdot(md, md)
    for _ in range(2):
        t = t + _bdot(t, xp)
        xp = _bdot(xp, xp)
    t = t + _bdot(t, xp)
    off16 = jnp.where((ii // 32 == jj // 32) & (bi == bj + 1), m, 0.0)
    t = t - _bdot(t, _bdot(off16, t))
    off32 = jnp.where((ii >= 32) & (jj < 32), m, 0.0)
    return t - _bdot(t, _bdot(off32, t))


def _gdn_chunk_kernel(q_ref, k_ref, v_ref, z_ref, cwq_ref, cwk_ref, cwv_ref, pcol_ref, grow_ref, nw_ref,
                      o_ref, s_ref, xq, xk, xv, S, u_s, w_s, qg_s, kd_s, a_s, o_s, *, nh, hb):
    R = q_ref.shape[0]
    nc = R // CHUNK
    nb = hb * nc
    hg = pl.program_id(1)
    r = pl.program_id(2)
    PADR = 8

    @pl.when(r == 0)
    def _():
        zero = jnp.zeros((PADR, hb * HEAD), F32)
        xq[0:PADR, :] = zero
        xk[0:PADR, :] = zero
        xv[0:PADR, :] = zero
        S[...] = jnp.zeros_like(S)

    @pl.when(r > 0)
    def _():
        xq[0:PADR, :] = xq[R:R + PADR, :]
        xk[0:PADR, :] = xk[R:R + PADR, :]
        xv[0:PADR, :] = xv[R:R + PADR, :]

    xq[PADR:PADR + R, :] = q_ref[...]
    xk[PADR:PADR + R, :] = k_ref[...]
    xv[PADR:PADR + R, :] = v_ref[...]

    def conv(xb, cw_ref):
        cw = cw_ref[...]
        y = xb[PADR - 3:PADR - 3 + R, :] * cw[0:1]
        for j in range(1, CONV_W):
            y = y + xb[PADR - 3 + j:PADR - 3 + j + R, :] * cw[j:j + 1]
        return _silu(y)

    def chunks(x):
        return jnp.concatenate([x[:, i * HEAD:(i + 1) * HEAD].reshape(nc, CHUNK, HEAD) for i in range(hb)], axis=0)

    def l2n(x):
        return x * lax.rsqrt(jnp.sum(x * x, axis=-1, keepdims=True) + EPS)

    q3 = l2n(chunks(conv(xq, cwq_ref))) * (HEAD ** -0.5)
    k3 = l2n(chunks(conv(xk, cwk_ref)))
    v3 = chunks(conv(xv, cwv_ref))
    pc = pcol_ref[...]
    beta3 = jnp.concatenate([_lane_pick(pc, hg * hb + i).reshape(nc, CHUNK, 1) for i in range(hb)], axis=0)
    gc3 = jnp.concatenate([_lane_pick(pc, nh + hg * hb + i).reshape(nc, CHUNK, 1) for i in range(hb)], axis=0)
    gr3 = grow_ref[...].reshape(nb, 1, CHUNK)
    ii = lax.broadcasted_iota(jnp.int32, (nb, CHUNK, CHUNK), 1)
    jj = lax.broadcasted_iota(jnp.int32, (nb, CHUNK, CHUNK), 2)
    decay = jnp.exp(jnp.where(ii >= jj, gc3 - gr3, NEG_INF))
    kb = k3 * beta3
    m = jnp.where(ii > jj, _bdot_nt(kb, k3) * decay, 0.0)
    t = _unit_lower_inverse(m, ii, jj)
    sol = _bdot(t, jnp.concatenate([v3 * beta3, kb * jnp.exp(gc3)], axis=2))
    u_s[...] = sol[:, :, :HEAD]
    w_s[...] = sol[:, :, HEAD:].astype(BF16)
    a_s[...] = jnp.where(ii >= jj, _bdot_nt(q3, k3) * decay, 0.0).astype(BF16)
    qg_s[...] = (q3 * jnp.exp(gc3)).astype(BF16)
    g_last = gc3[:, CHUNK - 1:CHUNK, :]
    kd_s[...] = (k3 * jnp.exp(g_last - gc3)).astype(BF16)
    egl = jnp.exp(g_last)

    for c in range(nc):
        for i in range(hb):
            idx = i * nc + c
            Sb = S[i]
            Sbf = Sb.astype(BF16)
            v_new = u_s[idx] - _dot(w_s[idx], Sbf)
            vnb = v_new.astype(BF16)
            o_s[idx] = _dot(qg_s[idx], Sbf) + _dot(a_s[idx], vnb)
            S[i] = Sb * egl[idx] + _dot_tn(kd_s[idx], vnb)

    nw = nw_ref[...]
    for i in range(hb):
        o = o_s[i * nc:(i + 1) * nc].reshape(R, HEAD)
        on = o * lax.rsqrt(jnp.mean(o * o, axis=-1, keepdims=True) + EPS) * nw
        o_ref[:, i * HEAD:(i + 1) * HEAD] = (on * _silu(z_ref[:, i * HEAD:(i + 1) * HEAD])).astype(o_ref.dtype)

    @pl.when(r == pl.num_programs(2) - 1)
    def _():
        s_ref[...] = S[...]


def _gdn_chunked(proj, conv_w, pcol, grow, norm_w, B, L, nh):
    M = B * L
    R = min(L, 512)
    nr = L // R
    nc = R // CHUNK
    hb = 4 if nh % 4 == 0 else nh
    ng = nh // hb
    W = hb * HEAD

    def col(off):
        return pl.BlockSpec((R, W), lambda b, g, r: (b * nr + r, off * ng + g))

    def cw(off):
        return pl.BlockSpec((CONV_W, W), lambda b, g, r: (0, off * ng + g))

    chunk_f32 = pltpu.VMEM((hb * nc, CHUNK, HEAD), F32)
    chunk_bf = pltpu.VMEM((hb * nc, CHUNK, HEAD), BF16)
    return pl.pallas_call(
        functools.partial(_gdn_chunk_kernel, nh=nh, hb=hb),
        grid=(B, ng, nr),
        in_specs=[col(0), col(1), col(2), col(3), cw(0), cw(1), cw(2),
                  pl.BlockSpec((R, HEAD), lambda b, g, r: (b * nr + r, 0)),
                  pl.BlockSpec((hb, nc, 1, CHUNK), lambda b, g, r: (g, b * nr + r, 0, 0)),
                  pl.BlockSpec((1, HEAD), lambda b, g, r: (0, 0))],
        out_specs=[pl.BlockSpec((R, W), lambda b, g, r: (b * nr + r, g)),
                   pl.BlockSpec((None, hb, HEAD, HEAD), lambda b, g, r: (b, g, 0, 0))],
        out_shape=[jax.ShapeDtypeStruct((M, nh * HEAD), BF16),
                   jax.ShapeDtypeStruct((B, nh, HEAD, HEAD), F32)],
        scratch_shapes=[pltpu.VMEM((R + 8, W), F32)] * 3 + [pltpu.VMEM((hb, HEAD, HEAD), F32),
                        chunk_f32, chunk_bf, chunk_bf, chunk_bf,
                        pltpu.VMEM((hb * nc, CHUNK, CHUNK), BF16), chunk_f32],
        compiler_params=_cparams(3, 48),
        name="gdn_chunked",
    )(proj, proj, proj, proj, conv_w, conv_w, conv_w, pcol, grow, norm_w)


def _gdn_step_kernel(q_ref, k_ref, v_ref, z_ref, cq_ref, ck_ref, cv_ref, cwq_ref, cwk_ref, cwv_ref,
                     g_ref, alog_ref, dtb_ref, nw_ref, s0_ref, o_ref, s_ref, *, nh):
    h = pl.program_id(1)

    def conv(c_ref, x_ref, cw_ref):
        cs, cw = c_ref[...], cw_ref[...]
        y = cs[0:1] * cw[0:1]
        for j in range(1, CONV_W - 1):
            y = y + cs[j:j + 1] * cw[j:j + 1]
        y = y + x_ref[...] * cw[CONV_W - 1:CONV_W]
        return _silu(y)

    def l2n(x):
        return x * lax.rsqrt(jnp.sum(x * x, axis=-1, keepdims=True) + EPS)

    q = l2n(conv(cq_ref, q_ref, cwq_ref)) * (HEAD ** -0.5)
    k = l2n(conv(ck_ref, k_ref, cwk_ref))
    v = conv(cv_ref, v_ref, cwv_ref)
    beta_all, g_all = _gdn_gates(g_ref[...], alog_ref[...], dtb_ref[...], nh)
    beta = _lane_pick(beta_all, h)
    g = _lane_pick(g_all, h)
    eg = jnp.exp(g)
    S0 = s0_ref[...]

    def rows8(x):
        return jnp.broadcast_to(x, (8, HEAD))

    kb = k * beta
    w = kb * eg
    v_new = v * beta - _dot_hi(rows8(w), S0)[0:1]
    a = jnp.sum(q * k, axis=-1, keepdims=True)
    o = _dot_hi(rows8(q * eg), S0)[0:1] + a * v_new
    s_ref[...] = S0 * eg + _dot_tn(rows8(k), rows8(v_new), precision=HI) * 0.125
    on = o * lax.rsqrt(jnp.mean(o * o, axis=-1, keepdims=True) + EPS) * nw_ref[...]
    o_ref[...] = on * _silu(z_ref[...])


def _gdn_step(proj, gates, conv_state, conv_w, alog, dtb, norm_w, s0, nh):
    B = proj.shape[0]
    proj3 = proj.reshape(B, 1, proj.shape[1])
    gates3 = gates.reshape(B, 1, HEAD)

    def col(off):
        return pl.BlockSpec((None, 1, HEAD), lambda b, h: (b, 0, off * nh + h))

    def cst(off):
        return pl.BlockSpec((None, CONV_W - 1, HEAD), lambda b, h: (b, 0, off * nh + h))

    def cw(off):
        return pl.BlockSpec((CONV_W, HEAD), lambda b, h: (0, off * nh + h))

    vec = pl.BlockSpec((1, nh), lambda b, h: (0, 0))
    o, s = pl.pallas_call(
        functools.partial(_gdn_step_kernel, nh=nh),
        grid=(B, nh),
        in_specs=[col(0), col(1), col(2), col(3), cst(0), cst(1), cst(2), cw(0), cw(1), cw(2),
                  pl.BlockSpec((None, 1, HEAD), lambda b, h: (b, 0, 0)), vec, vec,
                  pl.BlockSpec((1, HEAD), lambda b, h: (0, 0)),
                  pl.BlockSpec((None, None, HEAD, HEAD), lambda b, h: (b, h, 0, 0))],
        out_specs=[pl.BlockSpec((None, 1, HEAD), lambda b, h: (b, 0, h)),
                   pl.BlockSpec((None, None, HEAD, HEAD), lambda b, h: (b, h, 0, 0))],
        out_shape=[jax.ShapeDtypeStruct((B, 1, nh * HEAD), F32),
                   jax.ShapeDtypeStruct((B, nh, HEAD, HEAD), F32)],
        compiler_params=_cparams(2, 32),
        name="gdn_step",
    )(proj3, proj3, proj3, proj3, conv_state, conv_state, conv_state, conv_w, conv_w, conv_w,
      gates3, alog, dtb, norm_w, s0)
    return o.reshape(B, nh * HEAD), s


def _fcum_kernel(lf_ref, fcol_ref, frow_ref, carry, *, nh):
    R = lf_ref.shape[0]

    @pl.when(pl.program_id(1) == 0)
    def _():
        carry[...] = jnp.zeros_like(carry)

    ii = lax.broadcasted_iota(jnp.int32, (R, R), 0)
    jj = lax.broadcasted_iota(jnp.int32, (R, R), 1)
    tri = jnp.where(jj <= ii, 1.0, 0.0).astype(F32)
    f = _dot_hi(tri, lf_ref[...]) + carry[...]
    carry[...] = f[R - 1:R, :]
    fcol_ref[...] = f
    frow_ref[...] = f.T[0:nh, :]


def _fcum(logf, B, L, nh):
    R = 256
    nr = L // R
    return pl.pallas_call(
        functools.partial(_fcum_kernel, nh=nh),
        grid=(B, nr),
        in_specs=[pl.BlockSpec((R, HEAD), lambda b, r: (b * nr + r, 0))],
        out_specs=[pl.BlockSpec((R, HEAD), lambda b, r: (b * nr + r, 0)),
                   pl.BlockSpec((nh, R), lambda b, r: (0, b * nr + r))],
        out_shape=[jax.ShapeDtypeStruct((B * L, HEAD), F32), jax.ShapeDtypeStruct((nh, B * L), F32)],
        scratch_shapes=[pltpu.VMEM((1, HEAD), F32)],
        compiler_params=_cparams(2, 32),
        name="fcum",
    )(logf)


def _fox_prompt_kernel(q_ref, k_ref, v_ref, fcol_ref, frow_ref, o_ref, *, nq):
    T = q_ref.shape[0]
    h = pl.program_id(1)
    qi = pl.program_id(2)
    scale = HEAD ** -0.5

    def attend(n):
        q = q_ref[...]
        fq = _lane_pick(fcol_ref[...], h)
        fr = _row_pick(frow_ref[...], h)
        lo = (n - 1) * T
        ii = lax.broadcasted_iota(jnp.int32, (T, T), 0)
        jj = lax.broadcasted_iota(jnp.int32, (T, T), 1)
        s1 = _dot_nt(q, k_ref[lo:lo + T, :]) * scale + fq - fr[:, lo:lo + T]
        s1 = jnp.where(jj <= ii, s1, NEG_INF)
        m = jnp.max(s1, axis=-1, keepdims=True)
        if n > 1:
            s0 = _dot_nt(q, k_ref[0:lo, :]) * scale + fq - fr[:, 0:lo]
            m = jnp.maximum(m, jnp.max(s0, axis=-1, keepdims=True))
        p1 = jnp.exp(s1 - m)
        l = jnp.sum(p1, axis=-1, keepdims=True)
        acc = _dot(p1.astype(BF16), v_ref[lo:lo + T, :])
        if n > 1:
            p0 = jnp.exp(s0 - m)
            l = l + jnp.sum(p0, axis=-1, keepdims=True)
            acc = acc + _dot(p0.astype(BF16), v_ref[0:lo, :])
        o_ref[...] = (acc / l).astype(o_ref.dtype)

    for n in range(1, nq + 1):
        pl.when(qi == n - 1)(functools.partial(attend, n))


def _fox_prompt(q, k, v, fcol, frow, B, L, nh):
    T = min(L, 512)
    nq = L // T
    return pl.pallas_call(
        functools.partial(_fox_prompt_kernel, nq=nq),
        grid=(B, nh, nq),
        in_specs=[pl.BlockSpec((T, HEAD), lambda b, h, i: (b * nq + i, h)),
                  pl.BlockSpec((L, HEAD), lambda b, h, i: (b, h)),
                  pl.BlockSpec((L, HEAD), lambda b, h, i: (b, h)),
                  pl.BlockSpec((T, HEAD), lambda b, h, i: (b * nq + i, 0)),
                  pl.BlockSpec((nh, L), lambda b, h, i: (0, b))],
        out_specs=pl.BlockSpec((T, HEAD), lambda b, h, i: (b * nq + i, h)),
        out_shape=jax.ShapeDtypeStruct((B * L, nh * HEAD), BF16),
        compiler_params=_cparams(3, 48),
        name="fox_prompt",
    )(q, k, v, fcol, frow)


def _decode_bias_kernel(pt_ref, *refs, pg):
    lf_refs = refs[:pg]
    lfn_ref, b_ref, carry = refs[pg:]
    P = lf_refs[0].shape[0]

    @pl.when(pl.program_id(1) == 0)
    def _():
        carry[...] = lfn_ref[...]

    ii = lax.broadcasted_iota(jnp.int32, (P, P), 0)
    jj = lax.broadcasted_iota(jnp.int32, (P, P), 1)
    upper = jnp.where(jj > ii, 1.0, 0.0).astype(F32)
    for s in range(pg):
        lf = lf_refs[s][...]
        b_ref[pg - 1 - s] = _dot_hi(upper, lf) + carry[...]
        carry[...] = carry[...] + jnp.sum(lf, axis=0, keepdims=True)


def _decode_bias(cache_logf, page_table, logf_new):
    n_phys, P, nh = cache_logf.shape
    B, n_pages = page_table.shape
    pg = 8 if n_pages % 8 == 0 else 1
    ng = n_pages // pg
    lfn = logf_new.reshape(B, 1, nh)

    def page(s):
        return pl.BlockSpec((None, P, nh), lambda b, g, pt: (pt[b, n_pages - 1 - (g * pg + s)], 0, 0))

    return pl.pallas_call(
        functools.partial(_decode_bias_kernel, pg=pg),
        grid_spec=pltpu.PrefetchScalarGridSpec(
            num_scalar_prefetch=1,
            grid=(B, ng),
            in_specs=[page(s) for s in range(pg)] + [pl.BlockSpec((None, 1, nh), lambda b, g, pt: (b, 0, 0))],
            out_specs=pl.BlockSpec((None, pg, P, nh), lambda b, g, pt: (b, ng - 1 - g, 0, 0)),
            scratch_shapes=[pltpu.VMEM((1, nh), F32)]),
        out_shape=jax.ShapeDtypeStruct((B, n_pages, P, nh), F32),
        compiler_params=_cparams(2, 32),
        name="decode_bias",
    )(page_table, *([cache_logf] * pg), lfn)


def _fox_decode_kernel(pt_ref, q_ref, *refs, nh, pg):
    k_refs, v_refs = refs[:pg], refs[pg:2 * pg]
    b_ref, kn_ref, vn_ref, o_ref, m_s, l_s, acc_s = refs[2 * pg:]
    p = pl.program_id(1)
    scale = HEAD ** -0.5

    @pl.when(p == 0)
    def _():
        m_s[...] = jnp.full_like(m_s, NEG_INF)
        l_s[...] = jnp.zeros_like(l_s)
        acc_s[...] = jnp.zeros_like(acc_s)

    q = q_ref[...]
    qh, ql = _split(q)
    sub = lax.broadcasted_iota(jnp.int32, (nh, k_refs[0].shape[0]), 0)
    lane = lax.broadcasted_iota(jnp.int32, (nh, k_refs[0].shape[0]), 1)
    own = (lane % nh) == sub

    def scores(i):
        kh, kl = _split(k_refs[i][...])
        return _dot_nt(qh, kh) + (_dot_nt(qh, kl) + _dot_nt(ql, kh))

    ss = [jnp.where(own, scores(i) * scale + b_ref[i], NEG_INF) for i in range(pg)]
    m_old = m_s[...]
    m_new = m_old
    for s in ss:
        m_new = jnp.maximum(m_new, jnp.max(s, axis=-1, keepdims=True))
    alpha = jnp.exp(m_old - m_new)
    l_new = l_s[...] * alpha
    acc = acc_s[...] * alpha
    for i in range(pg):
        pr = jnp.exp(ss[i] - m_new)
        l_new = l_new + jnp.sum(pr, axis=-1, keepdims=True)
        acc = acc + _dot(pr.astype(BF16), v_refs[i][...].astype(BF16))
    l_s[...] = l_new
    acc_s[...] = acc
    m_s[...] = m_new

    @pl.when(p == pl.num_programs(1) - 1)
    def _():
        s_n = jnp.sum(q * kn_ref[...], axis=-1, keepdims=True) * scale
        m_f = jnp.maximum(m_s[...], s_n)
        a = jnp.exp(m_s[...] - m_f)
        pn = jnp.exp(s_n - m_f)
        l_f = l_s[...] * a + pn
        o_ref[...] = (acc_s[...] * a + pn * vn_ref[...]) / l_f


def _fox_decode(q, cache_k, cache_v, bias, k_new, v_new, page_table):
    n_phys, P, nh, dh = cache_k.shape
    B, n_pages = page_table.shape
    k2 = cache_k.reshape(n_phys, P * nh, dh)
    v2 = cache_v.reshape(n_phys, P * nh, dh)
    b4 = bias.reshape(B, n_pages, 1, P * nh)
    pg = 4 if n_pages % 4 == 0 else 1
    head_spec = pl.BlockSpec((None, nh, dh), lambda b, p, pt: (b, 0, 0))

    def page(i):
        return pl.BlockSpec((None, P * nh, dh), lambda b, p, pt: (pt[b, p * pg + i], 0, 0))

    pages = [page(i) for i in range(pg)]
    return pl.pallas_call(
        functools.partial(_fox_decode_kernel, nh=nh, pg=pg),
        grid_spec=pltpu.PrefetchScalarGridSpec(
            num_scalar_prefetch=1,
            grid=(B, n_pages // pg),
            in_specs=[head_spec] + pages + pages +
                     [pl.BlockSpec((None, pg, 1, P * nh), lambda b, p, pt: (b, p, 0, 0)), head_spec, head_spec],
            out_specs=head_spec,
            scratch_shapes=[pltpu.VMEM((nh, 1), F32), pltpu.VMEM((nh, 1), F32), pltpu.VMEM((nh, dh), F32)]),
        out_shape=jax.ShapeDtypeStruct((B, nh, dh), F32),
        compiler_params=_cparams(2, 40),
        name="fox_decode",
    )(page_table, q, *([k2] * pg), *([v2] * pg), b4, k_new, v_new)


def _router_kernel(u_ref, w_ref, b_ref, comb_ref, info_ref, rank_ref, carry):
    tm = u_ref.shape[0]

    @pl.when(pl.program_id(0) == 0)
    def _():
        carry[...] = jnp.zeros_like(carry)

    if u_ref.dtype == F32 and tm <= SPLIT_MAX_ROWS:
        (uh, ul), (wh, wl) = _split(u_ref[...]), _split(w_ref[...])
        lg = _dot(uh, wh) + (_dot(uh, wl) + _dot(ul, wh)) + b_ref[...]
    else:
        lg = _dot(u_ref[...].astype(BF16), w_ref[...].astype(BF16)) + b_ref[...]
    lane = lax.broadcasted_iota(jnp.int32, lg.shape, 1)
    lg = jnp.where(lane < N_EXPERTS, lg, NEG_INF)
    m1 = jnp.max(lg, axis=-1, keepdims=True)
    i1 = jnp.min(jnp.where(lg == m1, lane, HEAD), axis=-1, keepdims=True)
    lg2 = jnp.where(lane == i1, NEG_INF, lg)
    m2 = jnp.max(lg2, axis=-1, keepdims=True)
    i2 = jnp.min(jnp.where(lg2 == m2, lane, HEAD), axis=-1, keepdims=True)
    e = jnp.exp(m2 - m1)
    w1 = 1.0 / (1.0 + e)
    w2 = e / (1.0 + e)
    comb_ref[...] = jnp.where(lane == i1, w1, 0.0) + jnp.where(lane == i2, w2, 0.0)
    sel = jnp.where((lane == i1) | (lane == i2), 1.0, 0.0)
    ii = lax.broadcasted_iota(jnp.int32, (tm, tm), 0)
    jj = lax.broadcasted_iota(jnp.int32, (tm, tm), 1)
    before = jnp.where(jj < ii, 1.0, 0.0).astype(BF16)
    rank = _dot(before, sel.astype(BF16)) + carry[...]
    carry[...] = carry[...] + jnp.sum(sel, axis=0, keepdims=True)
    rank_ref[...] = jnp.broadcast_to(carry[...], rank_ref.shape)
    r1 = jnp.sum(jnp.where(lane == i1, rank, 0.0), axis=-1, keepdims=True)
    r2 = jnp.sum(jnp.where(lane == i2, rank, 0.0), axis=-1, keepdims=True)
    info = jnp.where(lane == 0, i1.astype(F32), 0.0) + jnp.where(lane == 1, i2.astype(F32), 0.0)
    info = info + jnp.where(lane == 2, w1, 0.0) + jnp.where(lane == 3, w2, 0.0)
    info_ref[...] = info + jnp.where(lane == 4, r1, 0.0) + jnp.where(lane == 5, r2, 0.0)


def _router(u, w_router, b_router):
    M, D = u.shape
    tm = min(M, 512)
    wp = jnp.pad(w_router, ((0, 0), (0, HEAD - N_EXPERTS)))
    bp = jnp.pad(b_router, (0, HEAD - N_EXPERTS)).reshape(1, HEAD)
    return pl.pallas_call(
        _router_kernel,
        grid=(M // tm,),
        in_specs=[pl.BlockSpec((tm, D), lambda i: (i, 0)),
                  pl.BlockSpec((D, HEAD), lambda i: (0, 0)),
                  pl.BlockSpec((1, HEAD), lambda i: (0, 0))],
        out_specs=[pl.BlockSpec((tm, HEAD), lambda i: (i, 0)),
                   pl.BlockSpec((tm, HEAD), lambda i: (i, 0)),
                   pl.BlockSpec((None, 8, HEAD), lambda i: (i, 0, 0))],
        out_shape=[jax.ShapeDtypeStruct((M, HEAD), F32), jax.ShapeDtypeStruct((M, HEAD), F32),
                   jax.ShapeDtypeStruct((M // tm, 8, HEAD), F32)],
        scratch_shapes=[pltpu.VMEM((1, HEAD), F32)],
        compiler_params=_cparams(1, 32),
        name="router",
    )(u, wp, bp)


MOE_TM = 512


def _moe_plan(info, counts, M):
    tm = MOE_TM
    cnt = counts[-1, 0, :N_EXPERTS].astype(jnp.int32)
    padded = ((cnt + tm - 1) // tm) * tm
    ends = jnp.cumsum(padded)
    off = ends - padded
    e1 = info[:, 0].astype(jnp.int32)
    e2 = info[:, 1].astype(jnp.int32)
    d1 = off[e1] + info[:, 4].astype(jnp.int32)
    d2 = off[e2] + info[:, 5].astype(jnp.int32)
    n_rows = 2 * M + N_EXPERTS * tm
    nt = n_rows // tm
    tok = jnp.arange(M, dtype=jnp.int32)
    src = jnp.zeros((n_rows,), jnp.int32).at[jnp.concatenate([d1, d2])].set(jnp.concatenate([tok, tok]))
    start = jnp.arange(nt, dtype=jnp.int32) * tm
    te_raw = jnp.sum((start[:, None] >= ends[None, :]).astype(jnp.int32), axis=1)
    valid = (te_raw < N_EXPERTS).astype(jnp.int32)
    last_e = jnp.max(jnp.where(cnt > 0, jnp.arange(N_EXPERTS, dtype=jnp.int32), 0))
    te = jnp.where(valid == 1, te_raw, last_e).astype(jnp.int32)
    first = jnp.concatenate([jnp.ones((1,), jnp.int32), (te[1:] != te[:-1]).astype(jnp.int32)])
    return dict(d1=d1, d2=d2, src=src.reshape(nt, 1, tm), te=te, first=first, valid=valid, n_rows=n_rows, nt=nt)


def _moe_gather_kernel(valid_ref, src_ref, nxt_ref, u_hbm, o_ref, buf, sem):
    tm = o_ref.shape[0]
    i = pl.program_id(0)
    n = pl.num_programs(0)
    slot = i % 2

    def issue(idx_ref, s):
        def one(r, c):
            pltpu.make_async_copy(u_hbm.at[pl.ds(idx_ref[0, r], 1)], buf.at[s, pl.ds(r, 1)], sem.at[s]).start()
            return c

        lax.fori_loop(0, tm, one, 0, unroll=8)

    @pl.when((i == 0) & (valid_ref[0] == 1))
    def _():
        issue(src_ref, 0)

    @pl.when((i + 1 < n) & (valid_ref[jnp.minimum(i + 1, n - 1)] == 1))
    def _():
        issue(nxt_ref, 1 - slot)

    @pl.when(valid_ref[i] == 1)
    def _():
        pltpu.make_async_copy(u_hbm.at[pl.ds(0, tm)], buf.at[slot], sem.at[slot]).wait()
        o_ref[...] = buf[slot].astype(o_ref.dtype)

    @pl.when(valid_ref[i] == 0)
    def _():
        o_ref[...] = jnp.zeros_like(o_ref)


def _moe_gather(u, plan):
    M, D = u.shape
    tm, nt = MOE_TM, plan['nt']
    return pl.pallas_call(
        _moe_gather_kernel,
        grid_spec=pltpu.PrefetchScalarGridSpec(
            num_scalar_prefetch=1,
            grid=(nt,),
            in_specs=[pl.BlockSpec((None, 1, tm), lambda i, va: (i, 0, 0), memory_space=pltpu.SMEM),
                      pl.BlockSpec((None, 1, tm), lambda i, va: (jnp.minimum(i + 1, nt - 1), 0, 0),
                                   memory_space=pltpu.SMEM),
                      pl.BlockSpec(memory_space=pl.ANY)],
            out_specs=pl.BlockSpec((tm, D), lambda i, va: (i, 0)),
            scratch_shapes=[pltpu.VMEM((2, tm, D), F32), pltpu.SemaphoreType.DMA((2,))]),
        out_shape=jax.ShapeDtypeStruct((plan['n_rows'], D), BF16),
        compiler_params=_cparams(1, 32),
        name="moe_gather",
    )(plan['valid'], plan['src'], plan['src'], u)


def _grouped_matmul(x, ws, plan, *, tn, n_blocks, epilogue, outs, name, vmem_mb=48, w_buffers=2):
    Mp, K = x.shape
    tm = MOE_TM
    nw, no = len(ws), len(outs)

    def kern(te_ref, first_ref, valid_ref, *refs):
        x_ref = refs[0]
        w_refs = refs[1:1 + nw]
        o_refs = refs[1 + nw:1 + nw + no]
        wbf = refs[1 + nw + no:]
        i = pl.program_id(1)

        @pl.when(first_ref[i] == 1)
        def _():
            for wr, wb in zip(w_refs, wbf):
                wb[...] = wr[...].astype(BF16)

        @pl.when(valid_ref[i] == 1)
        def _():
            xv = x_ref[...]
            accs = [_dot(xv, wb[...]) for wb in wbf]
            for o_ref, o in zip(o_refs, epilogue(accs)):
                o_ref[...] = o.astype(o_ref.dtype)

        @pl.when(valid_ref[i] == 0)
        def _():
            for o_ref in o_refs:
                o_ref[...] = jnp.zeros_like(o_ref)

    in_specs = [pl.BlockSpec((tm, K), lambda j, i, te, fi, va: (i, 0))]
    for arr, lead, off in ws:
        nl = len(lead) + 1
        in_specs.append(pl.BlockSpec(
            (None,) * nl + (K, tn),
            functools.partial(lambda j, i, te, fi, va, lead, off: lead + (te[i], 0, off + j), lead=tuple(lead), off=off),
            pipeline_mode=pl.Buffered(w_buffers)))
    return pl.pallas_call(
        kern,
        grid_spec=pltpu.PrefetchScalarGridSpec(
            num_scalar_prefetch=3,
            grid=(n_blocks, Mp // tm),
            in_specs=in_specs,
            out_specs=[pl.BlockSpec((tm, tn), lambda j, i, te, fi, va: (i, j)) for _ in outs],
            scratch_shapes=[pltpu.VMEM((K, tn), BF16) for _ in ws]),
        out_shape=outs,
        compiler_params=_cparams(2, vmem_mb),
        name=name,
    )(plan['te'], plan['first'], plan['valid'], x, *[arr for arr, _, _ in ws])


def _moe_combine_kernel(d1_ref, d2_ref, info_ref, h_ref, gate_ref, y_hbm, hn_ref, buf, sem):
    tl = h_ref.shape[1]

    def issue(r, c):
        pltpu.make_async_copy(y_hbm.at[pl.ds(d1_ref[0, r], 1)], buf.at[0, pl.ds(r, 1)], sem).start()
        pltpu.make_async_copy(y_hbm.at[pl.ds(d2_ref[0, r], 1)], buf.at[1, pl.ds(r, 1)], sem).start()
        return c

    lax.fori_loop(0, tl, issue, 0)
    pltpu.make_async_copy(y_hbm.at[pl.ds(0, tl)], buf.at[0], sem).wait()
    pltpu.make_async_copy(y_hbm.at[pl.ds(0, tl)], buf.at[1], sem).wait()
    info = info_ref[...]
    y = info[:, 2:3] * buf[0] + info[:, 3:4] * buf[1]
    hn_ref[0] = h_ref[0] + gate_ref[0] * y


def _moe_combine(y_sorted, plan, info, h, gate):
    B, L, D = h.shape
    tl = 256
    nl = L // tl
    d1 = plan['d1'].reshape(B * nl, 1, tl)
    d2 = plan['d2'].reshape(B * nl, 1, tl)
    smem_rows = pl.BlockSpec((None, 1, tl), lambda b, t: (b * nl + t, 0, 0), memory_space=pltpu.SMEM)
    return pl.pallas_call(
        _moe_combine_kernel,
        grid=(B, nl),
        in_specs=[smem_rows, smem_rows,
                  pl.BlockSpec((tl, HEAD), lambda b, t: (b * nl + t, 0)),
                  pl.BlockSpec((1, tl, D), lambda b, t: (b, t, 0)),
                  pl.BlockSpec((1, 1, D), lambda b, t: (b, 0, 0)),
                  pl.BlockSpec(memory_space=pl.ANY)],
        out_specs=pl.BlockSpec((1, tl, D), lambda b, t: (b, t, 0)),
        out_shape=jax.ShapeDtypeStruct((B, L, D), F32),
        scratch_shapes=[pltpu.VMEM((2, tl, D), F32), pltpu.SemaphoreType.DMA(())],
        compiler_params=_cparams(2, 32),
        name="moe_combine",
    )(d1, d2, info, h, gate, y_sorted)


def _moe_sparse(u, h, gate, w_router, b_router, w_gu, w_down, j):
    M, D = u.shape
    d_ffe = w_down.shape[2]
    comb, info, counts = _router(u, w_router, b_router)
    plan = _moe_plan(info, counts, M)
    xs = _moe_gather(u, plan)
    half = d_ffe // 2
    tn = half if (d_ffe % 2 == 0 and half % HEAD == 0) else d_ffe
    nb = d_ffe // tn
    (act,) = _grouped_matmul(xs, [(w_gu, (j,), 0), (w_gu, (j,), nb)], plan, tn=tn, n_blocks=nb,
                             epilogue=lambda accs: (_silu(accs[0]) * accs[1],),
                             outs=[jax.ShapeDtypeStruct((plan['n_rows'], d_ffe), BF16)], name="moe_gu",
                             vmem_mb=56, w_buffers=1)
    tn_d = 1024 if D % 1024 == 0 else D
    (ys,) = _grouped_matmul(act, [(w_down, (j,), 0)], plan, tn=tn_d, n_blocks=D // tn_d,
                            epilogue=lambda accs: (accs[0],),
                            outs=[jax.ShapeDtypeStruct((plan['n_rows'], D), F32)], name="moe_down", vmem_mb=52)
    return _moe_combine(ys, plan, info, h, gate)


def _ffn_tn(d_ff):
    for tn in (1024, 512, 256, 128):
        if d_ff % tn == 0:
            return tn
    raise ValueError(d_ff)


def _trunk(x, mods, mod_kv, p, *, prompt, conv_bufs=None, gdn_states=None,
           cache_k=None, cache_v=None, cache_logf=None, page_table=None):
    B, L, D = x.shape
    M = B * L
    depth = p['w_ada'].shape[0]
    n_a = p['gdn_w_in'].shape[0]
    nh = D // HEAD
    d_ff = p['ffn_w_down'].shape[1]
    d_ffe = p['moe_w_down'].shape[2]
    tn_d = 1024 if D % 1024 == 0 else D

    if prompt:
        def mod3(v):
            return v.reshape(B, 1, D)
        h = x
    else:
        def mod3(v):
            return v.reshape(1, B, D)
        h = x.reshape(1, B, D)

    def rows(a):
        return a.reshape(M, a.shape[-1])

    def as_h(a):
        return a.reshape(h.shape)

    def gain(v):
        return v.reshape(1, -1)

    new_conv, new_gdn = [], []
    k_new = v_new = logf_new = None
    y = gate = None
    act_dtype = BF16 if prompt else F32
    for layer in range(depth):
        sh_m, sc_m, g_m, sh_f, sc_f, g_f = [mod3(mods[layer][:, i * D:(i + 1) * D]) for i in range(6)]
        if y is None:
            u = _resmod(h, gain=gain(p['norm_mix'][layer]), shift=sh_m, scale=sc_m, u_dtype=act_dtype)
        else:
            h, u = _resmod(h, y, gate, gain=gain(p['norm_mix'][layer]), shift=sh_m, scale=sc_m, u_dtype=act_dtype)
        u2 = rows(u)
        if layer < n_a:
            w_in = p['gdn_w_in']
            proj = _mm_plain(u2, w_in, (layer,), n_cols=4 * nh * HEAD, tn=tn_d)
            gates = _mm_plain(u2, w_in, (layer,), n_cols=HEAD, col0=4 * nh, tn=HEAD, mask_cols=2 * nh)
            alog = p['gdn_a_log'][layer].reshape(1, nh)
            dtb = p['gdn_dt_bias'][layer].reshape(1, nh)
            nw = p['gdn_norm'][layer].reshape(1, HEAD)
            cw = p['gdn_conv_w'][layer]
            n_conv = 3 * nh * HEAD
            if prompt:
                pcol, grow = _gdn_prep(gates, alog, dtb, nh)
                o, s_new = _gdn_chunked(proj, cw, pcol, grow, nw, B, L, nh)
                new_conv.append(proj.reshape(B, L, -1)[:, L - (CONV_W - 1):, :n_conv])
            else:
                o, s_new = _gdn_step(proj, gates, conv_bufs[layer], cw, alog, dtb, nw, gdn_states[layer], nh)
                new_conv.append(jnp.concatenate([conv_bufs[layer][:, 1:], proj[:, None, :n_conv]], axis=1))
            new_gdn.append(s_new)
            y = _mm_plain(o, p['gdn_w_out'], (layer,), n_cols=D, tn=tn_d)
        else:
            b = layer - n_a
            qn = gain(p['fox_q_norm'][b])
            if prompt:
                (q,) = _mm_headnorm(u2, p['fox_w_q'], (b,), qn, n_cols=D, tn=tn_d, out_dtypes=(BF16,))
                o = _fox_prompt(q, kbf, vbf, fcol, frow, B, L, nh)
            else:
                (q,) = _mm_headnorm(u2, p['fox_w_q'], (b,), qn, n_cols=D, tn=tn_d, out_dtypes=(F32,))
                o = _fox_decode(q.reshape(B, nh, HEAD), cache_k, cache_v, dec_bias,
                                k_new.reshape(B, nh, HEAD), v_new.reshape(B, nh, HEAD), page_table)
                o = o.reshape(B, nh * HEAD)
            y = _mm_plain(o, p['fox_w_o'], (b,), n_cols=D, tn=tn_d)
        sparse = prompt and layer % 2 == 1
        h, u = _resmod(h, as_h(y), g_m, gain=gain(p['norm_ffn'][layer]), shift=sh_f, scale=sc_f,
                       u_dtype=F32 if sparse else act_dtype)
        u2 = rows(u)
        j = layer // 2
        if sparse:
            h = _moe_sparse(u2, h, g_f, p['moe_w_router'][j], p['moe_b_router'][j], p['moe_w_gu'], p['moe_w_down'], j)
            y = None
        elif layer % 2 == 0:
            act = _mm_swiglu(u2, p['ffn_w_gu'], (j,), d_ff=d_ff, tn=min(512, _ffn_tn(d_ff)))
            y = _mm_plain(act, p['ffn_w_down'], (j,), n_cols=D, tn=512 if D % 512 == 0 else D)
        else:
            comb, _, _ = _router(u2, p['moe_w_router'][j], p['moe_b_router'][j])
            y = None
            for e in range(N_EXPERTS):
                act = _mm_swiglu(u2, p['moe_w_gu'], (j, e), d_ff=d_ffe, tn=min(512, _ffn_tn(d_ffe)))
                y = _mm_moe_down(act, p['moe_w_down'], (j, e), comb, e, y, tn=512 if D % 512 == 0 else D)
        gate = g_f
        if y is not None:
            y = as_h(y)
        if layer == n_a - 1:
            if y is not None:
                h = _resmod(h, y, gate)
                y = None
            shift, scale = mod3(mod_kv[:, :D]), mod3(mod_kv[:, D:])
            ukv = rows(_resmod(h, gain=gain(p['kv_norm']), shift=shift, scale=scale, u_dtype=act_dtype))
            w_kvf = p['w_kvf']
            k_new, kbf = _mm_headnorm(ukv, w_kvf, (), gain(p['k_norm']), n_cols=D, tn=tn_d, out_dtypes=(F32, BF16))
            v_new, vbf = _matmul(ukv, [(w_kvf, (), D // tn_d)], tm=_pick_tm(M, D), tn=tn_d, n_blocks=D // tn_d,
                                 epilogue=lambda accs, ev: (accs[0], accs[0]),
                                 outs=[(jax.ShapeDtypeStruct((M, D), dt), (_pick_tm(M, D), tn_d), lambda j, i: (i, j))
                                       for dt in (F32, BF16)])
            b_f = jnp.pad(p['b_f'], (0, HEAD - nh)).reshape(1, HEAD)
            logf_pad = _log_forget(ukv, w_kvf, b_f, col0=2 * D // HEAD, nh=nh)
            logf_new = logf_pad[:, :nh]
            if prompt:
                fcol, frow = _fcum(logf_pad, B, L, nh)
            else:
                dec_bias = _decode_bias(cache_logf, page_table, logf_new)
    if y is not None:
        h = _resmod(h, y, gate)
    return (h.reshape(B, L, D), k_new.reshape(B, L, nh, HEAD), v_new.reshape(B, L, nh, HEAD),
            logf_new.reshape(B, L, nh), jnp.stack(new_gdn), jnp.stack(new_conv))


def _log_forget(ukv, w_kvf, b_f, *, col0, nh):
    M, K = ukv.shape
    tm = _pick_tm(M, K)

    def epi(accs, ev):
        lane = lax.broadcasted_iota(jnp.int32, accs[0].shape, 1)
        return (jnp.where(lane < nh, -_softplus(-(accs[0] + ev[0])), 0.0),)

    (out,) = _matmul(ukv, [(w_kvf, (), col0)], tm=tm, tn=HEAD, n_blocks=1, epilogue=epi,
                     outs=[(jax.ShapeDtypeStruct((M, HEAD), F32), (tm, HEAD), lambda j, i: (i, 0))],
                     extras=[(b_f, (1, HEAD), lambda j, i: (0, 0))])
    return out


def kernel(x_prompt, x_sample, state_gdn, state_conv, cache_k, cache_v, cache_logf, page_table, c_prompt, c_sample, w_ada, b_ada, norm_mix, norm_ffn, gdn_w_in, gdn_conv_w, gdn_a_log, gdn_dt_bias, gdn_norm, gdn_w_out, kv_norm, w_ada_kv, b_ada_kv, w_kvf, b_f, k_norm, fox_w_q, fox_q_norm, fox_w_o, ffn_w_gu, ffn_w_down, moe_w_router, moe_b_router, moe_w_gu, moe_w_down):
    p = dict(w_ada=w_ada, b_ada=b_ada, norm_mix=norm_mix, norm_ffn=norm_ffn, gdn_w_in=gdn_w_in,
             gdn_conv_w=gdn_conv_w, gdn_a_log=gdn_a_log, gdn_dt_bias=gdn_dt_bias, gdn_norm=gdn_norm,
             gdn_w_out=gdn_w_out, kv_norm=kv_norm, w_ada_kv=w_ada_kv, b_ada_kv=b_ada_kv, w_kvf=w_kvf,
             b_f=b_f, k_norm=k_norm, fox_w_q=fox_w_q, fox_q_norm=fox_q_norm, fox_w_o=fox_w_o,
             ffn_w_gu=ffn_w_gu, ffn_w_down=ffn_w_down, moe_w_router=moe_w_router,
             moe_b_router=moe_b_router, moe_w_gu=moe_w_gu, moe_w_down=moe_w_down)
    bp, bs = c_prompt.shape[0], c_sample.shape[0]
    D = c_prompt.shape[1]
    depth = w_ada.shape[0]
    n_rows = -(-(bp + bs) // 8) * 8
    c_all = jnp.pad(jnp.concatenate([c_prompt, c_sample], axis=0), ((0, n_rows - bp - bs), (0, 0)))
    tn_a = 1024 if D % 1024 == 0 else D
    mods = [_mm_plain(c_all, w_ada, (l,), n_cols=6 * D, tn=tn_a, bias=b_ada[l].reshape(1, -1), silu_x=True)
            for l in range(depth)]
    mods_p = [m[:bp] for m in mods]
    mods_s = [m[bp:bp + bs] for m in mods]
    mod_kv = _mm_plain(c_all, w_ada_kv, (), n_cols=2 * D, tn=tn_a, bias=b_ada_kv.reshape(1, -1), silu_x=True)

    y_p, k_p, v_p, f_p, s_p, cv_p = _trunk(x_prompt, mods_p, mod_kv[:bp], p, prompt=True)
    y_s, k_s, v_s, f_s, s_s, cv_s = _trunk(x_sample, mods_s, mod_kv[bp:bp + bs], p, prompt=False,
                                           conv_bufs=state_conv, gdn_states=state_gdn, cache_k=cache_k,
                                           cache_v=cache_v, cache_logf=cache_logf, page_table=page_table)
    return (y_p, y_s, k_p, v_p, f_p, s_p, cv_p, k_s, v_s, f_s, s_s, cv_s)
```
